```python
import math
import jax, jax.numpy as jnp
from jax import lax
import numpy as np

D_MODEL = 1024
BATCH = 32
SEQ = 2048
DEPTH = 2
DEC_BATCH = 32
DEC_SEQ = 64
PAST_LEN = 2048

CHUNK = 64
HEAD_DIM = 64
ROT_DIM = HEAD_DIM // 4
ROPE_THETA = 500000.0
A_HEADS = 4
A_LEFT_CHUNKS = 8
A_WINDOW = A_LEFT_CHUNKS * CHUNK
A_BAND = A_WINDOW + CHUNK
REL_CLIP = 128
B_HEADS = 4
IDX_HEADS = 8
IDX_DIM = 64
TOPK_MAX = 256
C_HEADS = 4
N_MEM = 256
MEM_HEADS = 4
MEM_HEAD_DIM = D_MODEL // MEM_HEADS
D_FF = ((8 * D_MODEL // 3 + 127) // 128) * 128
QBLOCK = 128
N_BRANCH = 3
ALPHA = (2.0 * DEPTH) ** 0.25
BETA = (8.0 * DEPTH) ** -0.25
LN_EPS = 1e-5
SUBLN_EPS = 1e-5
A_WIDTH = A_HEADS * HEAD_DIM
B_WIDTH = B_HEADS * HEAD_DIM
C_WIDTH = C_HEADS * 2 * HEAD_DIM
IN_SIZES = (A_WIDTH, A_WIDTH, A_WIDTH,
            B_WIDTH, HEAD_DIM, HEAD_DIM, IDX_HEADS * IDX_DIM, IDX_DIM, IDX_HEADS,
            C_WIDTH, C_WIDTH, C_WIDTH,
            N_BRANCH * D_MODEL)
IN_COLS = sum(IN_SIZES)

kernel_name = 'hybrid_streaming_encoder_step'


def layer_norm(x, g, b):
    xf = x.astype(jnp.float32)
    mu = jnp.mean(xf, axis=-1, keepdims=True)
    var = jnp.mean(jnp.square(xf - mu), axis=-1, keepdims=True)
    return ((xf - mu) * lax.rsqrt(var + LN_EPS) * g.astype(jnp.float32) + b.astype(jnp.float32)).astype(x.dtype)


def swiglu(x, w_gu, w_d):
    g, u = jnp.split(x @ w_gu, 2, axis=-1)
    return (jax.nn.silu(g) * u) @ w_d


def rope(x, pos):
    half = ROT_DIM // 2
    inv_freq = ROPE_THETA ** (-jnp.arange(half, dtype=jnp.float32) / half)
    ang = pos.astype(jnp.float32)[:, None] * inv_freq
    ang = ang.reshape((1, pos.shape[0]) + (1,) * (x.ndim - 3) + (half,))
    cos, sin = jnp.cos(ang), jnp.sin(ang)
    xr = x[..., :ROT_DIM].astype(jnp.float32)
    x1, x2 = xr[..., :half], xr[..., half:]
    rot = jnp.concatenate([x1 * cos - x2 * sin, x2 * cos + x1 * sin], axis=-1)
    return jnp.concatenate([rot.astype(x.dtype), x[..., ROT_DIM:]], axis=-1)


def rel_bias_lookup(rel_bias, dist):
    idx = jnp.clip(dist, -REL_CLIP, REL_CLIP) + REL_CLIP
    return rel_bias[:, idx].astype(jnp.float32)


def band_attn_prompt(q, k, v, rel_bias):
    B, S = q.shape[:2]
    nc = S // CHUNK
    qc = q.reshape(B, nc, CHUNK, A_HEADS, HEAD_DIM)
    padw = ((0, 0), (A_WINDOW, 0), (0, 0), (0, 0))
    kp = jnp.pad(k, padw).reshape(B, nc + A_LEFT_CHUNKS, CHUNK, A_HEADS, HEAD_DIM)
    vp = jnp.pad(v, padw).reshape(B, nc + A_LEFT_CHUNKS, CHUNK, A_HEADS, HEAD_DIM)
    kb = jnp.concatenate([kp[:, j:j + nc] for j in range(A_LEFT_CHUNKS + 1)], axis=2)
    vb = jnp.concatenate([vp[:, j:j + nc] for j in range(A_LEFT_CHUNKS + 1)], axis=2)
    dist = jnp.arange(CHUNK)[:, None] + A_WINDOW - jnp.arange(A_BAND)[None, :]
    bias = rel_bias_lookup(rel_bias, dist)
    s = jnp.einsum('bcqhd,bckhd->bchqk', qc, kb).astype(jnp.float32) * HEAD_DIM ** -0.5 + bias
    valid = (jnp.arange(nc)[:, None] + jnp.arange(A_BAND)[None, :] // CHUNK) >= A_LEFT_CHUNKS
    s = jnp.where(valid[None, :, None, None, :], s, -jnp.inf)
    p = jax.nn.softmax(s, axis=-1)
    o = jnp.einsum('bchqk,bckhd->bcqhd', p.astype(vb.dtype), vb)
    return o.reshape(B, S, A_HEADS, HEAD_DIM)


def band_attn_step(q, k, v, pk, pv, rel_bias, qpos):
    W = pk.shape[1]
    kk = jnp.concatenate([pk, k], axis=1)
    vv = jnp.concatenate([pv, v], axis=1)
    kpos = jnp.concatenate([qpos[0] - W + jnp.arange(W, dtype=qpos.dtype), qpos])
    bias = rel_bias_lookup(rel_bias, qpos[:, None] - kpos[None, :])
    s = jnp.einsum('bqhd,bkhd->bhqk', q, kk).astype(jnp.float32) * HEAD_DIM ** -0.5 + bias
    p = jax.nn.softmax(s, axis=-1)
    return jnp.einsum('bhqk,bkhd->bqhd', p.astype(vv.dtype), vv)


def dsa_block(qpos_b, q_b, iq_b, iw_b, k, v, ik, kpos, topk):
    isc = jax.nn.relu(jnp.einsum('bqhe,bke->bqhk', iq_b, ik).astype(jnp.float32))
    score = jnp.einsum('bqh,bqhk->bqk', iw_b.astype(jnp.float32), isc)
    adm = (kpos // CHUNK)[None, :] <= (qpos_b // CHUNK)[:, None]
    score = jnp.where(adm[None], score, -jnp.inf)
    top_val, top_idx = lax.top_k(score, topk)
    gather = jax.vmap(lambda rows, idx: rows[idx])
    k_sel = gather(k, top_idx)
    v_sel = gather(v, top_idx)
    s = jnp.einsum('bqhd,bqkd->bqhk', q_b, k_sel).astype(jnp.float32) * HEAD_DIM ** -0.5
    s = jnp.where(jnp.isfinite(top_val)[:, :, None, :], s, -jnp.inf)
    p = jax.nn.softmax(s, axis=-1)
    return jnp.einsum('bqhk,bqkd->bqhd', p.astype(v_sel.dtype), v_sel)


def diff_attn_block(qpos_b, q_b, k, v, kpos, lam):
    s = jnp.einsum('bqhid,bkhid->bhiqk', q_b, k).astype(jnp.float32) * HEAD_DIM ** -0.5
    adm = (kpos // CHUNK)[None, :] <= (qpos_b // CHUNK)[:, None]
    s = jnp.where(adm, s, -jnp.inf)
    p = jax.nn.softmax(s, axis=-1)
    a = p[:, :, 0] - lam * p[:, :, 1]
    return jnp.einsum('bhqk,bkhe->bqhe', a.astype(v.dtype), v)


def over_query_blocks(fn, qpos, *qs):
    T = qpos.shape[0]
    if T <= QBLOCK:
        return fn(qpos, *qs)
    nb = T // QBLOCK

    def to_blocks(a):
        return jnp.moveaxis(a.reshape((a.shape[0], nb, QBLOCK) + a.shape[2:]), 1, 0)

    out = lax.map(lambda args: fn(*args), (qpos.reshape(nb, QBLOCK),) + tuple(to_blocks(a) for a in qs))
    out = jnp.moveaxis(out, 0, 1)
    return out.reshape((out.shape[0], T) + out.shape[3:])


def memory_attn(x, mem_k, mem_v, w_q, w_o):
    B, T, _ = x.shape
    q = (x @ w_q).reshape(B, T, MEM_HEADS, MEM_HEAD_DIM)
    s = jnp.einsum('bqhd,bkhd->bhqk', q, mem_k).astype(jnp.float32) * MEM_HEAD_DIM ** -0.5
    p = jax.nn.softmax(s, axis=-1)
    o = jnp.einsum('bhqk,bkhd->bqhd', p.astype(mem_v.dtype), mem_v)
    return o.reshape(B, T, D_MODEL) @ w_o


def trunk_layer(x, qpos, past, mem_k, mem_v, l, ln_g, ln_b, f1_gu, f1_d, f2_gu, f2_d,
                w_in, rel_bias, lam_p, subln_g, wb_a, wb_b, wb_c, w_out, wm_q, wm_o):
    B, T, _ = x.shape
    f32 = jnp.float32
    x = layer_norm(ALPHA * x + 0.5 * swiglu(x, f1_gu, f1_d), ln_g[0], ln_b[0])
    z = x @ w_in
    (aq, ak, av, bq, bk, bv, iq, ik, iw, cq, ck, cv, gates) = jnp.split(
        z, np.cumsum(IN_SIZES)[:-1].tolist(), axis=-1)
    hs = (B, T, A_HEADS, HEAD_DIM)
    aq, ak, av = aq.reshape(hs), ak.reshape(hs), av.reshape(hs)
    bq = rope(bq.reshape(B, T, B_HEADS, HEAD_DIM), qpos)
    bk = rope(bk, qpos)
    iq = rope(iq.reshape(B, T, IDX_HEADS, IDX_DIM), qpos)
    ik = rope(ik, qpos)
    iw = iw * (IDX_HEADS ** -0.5)
    cq = rope(cq.reshape(B, T, C_HEADS, 2, HEAD_DIM), qpos)
    ck = rope(ck.reshape(B, T, C_HEADS, 2, HEAD_DIM), qpos).reshape(B, T, C_HEADS, 2 * HEAD_DIM)
    cv = cv.reshape(B, T, C_HEADS, 2 * HEAD_DIM)
    if past is None:
        ya = band_attn_prompt(aq, ak, av, rel_bias)
        keep = min(A_WINDOW, T)
        a_new_k, a_new_v = ak[:, T - keep:], av[:, T - keep:]
        kpos = qpos
        b_k_all, b_v_all, i_k_all, c_k_all, c_v_all = bk, bv, ik, ck, cv
    else:
        pa_k, pa_v, pb_k, pb_v, pb_i, pc_k, pc_v = past
        ya = band_attn_step(aq, ak, av, pa_k, pa_v, rel_bias, qpos)
        a_new_k, a_new_v = ak, av
        kpos = jnp.arange(pb_k.shape[1] + T, dtype=qpos.dtype)
        b_k_all = jnp.concatenate([pb_k, bk], axis=1)
        b_v_all = jnp.concatenate([pb_v, bv], axis=1)
        i_k_all = jnp.concatenate([pb_i, ik], axis=1)
        c_k_all = jnp.concatenate([pc_k, ck], axis=1)
        c_v_all = jnp.concatenate([pc_v, cv], axis=1)
    topk = min(TOPK_MAX, kpos.shape[0] // 4)
    yb = over_query_blocks(
        lambda qp, q_, iq_, iw_: dsa_block(qp, q_, iq_, iw_, b_k_all, b_v_all, i_k_all, kpos, topk),
        qpos, bq, iq, iw)
    lam_init = 0.8 - 0.6 * math.exp(-0.3 * l)
    lp = lam_p.astype(f32)
    lam = jnp.exp(jnp.sum(lp[0] * lp[1])) - jnp.exp(jnp.sum(lp[2] * lp[3])) + lam_init
    c_k5 = c_k_all.reshape(c_k_all.shape[:3] + (2, HEAD_DIM))
    yc = over_query_blocks(lambda qp, q_: diff_attn_block(qp, q_, c_k5, c_v_all, kpos, lam), qpos, cq)
    ycf = yc.astype(f32)
    yc = (ycf * lax.rsqrt(jnp.mean(ycf * ycf, axis=-1, keepdims=True) + SUBLN_EPS)
          * subln_g.astype(f32) * (1.0 - lam_init)).astype(x.dtype)
    g_a, g_b, g_c = jnp.split(jax.nn.sigmoid(gates.astype(f32)).astype(x.dtype), N_BRANCH, axis=-1)
    merged = (g_a * (ya.reshape(B, T, A_WIDTH) @ wb_a)
              + g_b * (yb.reshape(B, T, B_WIDTH) @ wb_b)
              + g_c * (yc.reshape(B, T, C_WIDTH) @ wb_c))
    x = layer_norm(ALPHA * x + merged @ w_out, ln_g[1], ln_b[1])
    x = layer_norm(ALPHA * x + memory_attn(x, mem_k, mem_v, wm_q, wm_o), ln_g[2], ln_b[2])
    x = layer_norm(ALPHA * x + 0.5 * swiglu(x, f2_gu, f2_d), ln_g[3], ln_b[3])
    return x, (a_new_k, a_new_v, bk, bv, ik, ck, cv)


def setup_inputs(seed: int = 0) -> dict:
    key = jax.random.key(seed)
    ks = iter(jax.random.split(key, 40))

    def nrm(shape, scale):
        return scale * jax.random.normal(next(ks), shape, jnp.float32)

    win = min(A_WINDOW, PAST_LEN)
    D = D_MODEL
    return {
        'x_prompt': nrm((BATCH, SEQ, D), 1.0),
        'x_sample': nrm((DEC_BATCH, DEC_SEQ, D), 1.0),
        'cache_a_k': nrm((DEPTH, DEC_BATCH, win, A_HEADS, HEAD_DIM), 1.0),
        'cache_a_v': nrm((DEPTH, DEC_BATCH, win, A_HEADS, HEAD_DIM), 1.0),
        'cache_b_k': nrm((DEPTH, DEC_BATCH, PAST_LEN, HEAD_DIM), 1.0),
        'cache_b_v': nrm((DEPTH, DEC_BATCH, PAST_LEN, HEAD_DIM), 1.0),
        'cache_b_idx': nrm((DEPTH, DEC_BATCH, PAST_LEN, IDX_DIM), 1.0),
        'cache_c_k': nrm((DEPTH, DEC_BATCH, PAST_LEN, C_HEADS, 2 * HEAD_DIM), 1.0),
        'cache_c_v': nrm((DEPTH, DEC_BATCH, PAST_LEN, C_HEADS, 2 * HEAD_DIM), 1.0),
        'cache_mem_k': nrm((DEPTH, DEC_BATCH, N_MEM, MEM_HEADS, MEM_HEAD_DIM), 1.0),
        'cache_mem_v': nrm((DEPTH, DEC_BATCH, N_MEM, MEM_HEADS, MEM_HEAD_DIM), 1.0),
        'mem_prompt': nrm((BATCH, N_MEM, D), 1.0),
        'ln_g': 1.0 + nrm((DEPTH, 4, D), 0.02),
        'ln_b': nrm((DEPTH, 4, D), 0.02),
        'ffn1_w_gu': nrm((DEPTH, D, 2 * D_FF), D ** -0.5),
        'ffn1_w_d': nrm((DEPTH, D_FF, D), BETA * D_FF ** -0.5),
        'ffn2_w_gu': nrm((DEPTH, D, 2 * D_FF), D ** -0.5),
        'ffn2_w_d': nrm((DEPTH, D_FF, D), BETA * D_FF ** -0.5),
        'w_in': nrm((DEPTH, D, IN_COLS), D ** -0.5),
        'a_rel_bias': nrm((DEPTH, A_HEADS, 2 * REL_CLIP + 1), 0.2),
        'c_lambda': nrm((DEPTH, 4, HEAD_DIM), 0.1),
        'c_subln_g': 1.0 + nrm((DEPTH, 2 * HEAD_DIM), 0.02),
        'w_branch_a': nrm((DEPTH, A_WIDTH, D), A_WIDTH ** -0.5),
        'w_branch_b': nrm((DEPTH, B_WIDTH, D), B_WIDTH ** -0.5),
        'w_branch_c': nrm((DEPTH, C_WIDTH, D), C_WIDTH ** -0.5),
        'w_out': nrm((DEPTH, D, D), BETA * D ** -0.5),
        'w_mem_q': nrm((DEPTH, D, D), D ** -0.5),
        'w_mem_k': nrm((DEPTH, D, D), D ** -0.5),
        'w_mem_v': nrm((DEPTH, D, D), D ** -0.5),
        'w_mem_o': nrm((DEPTH, D, D), BETA * D ** -0.5),
    }


def reference(x_prompt, x_sample, cache_a_k, cache_a_v, cache_b_k, cache_b_v, cache_b_idx,
              cache_c_k, cache_c_v, cache_mem_k, cache_mem_v, mem_prompt,
              ln_g, ln_b, ffn1_w_gu, ffn1_w_d, ffn2_w_gu, ffn2_w_d, w_in, a_rel_bias,
              c_lambda, c_subln_g, w_branch_a, w_branch_b, w_branch_c, w_out,
              w_mem_q, w_mem_k, w_mem_v, w_mem_o):
    Bp, S, _ = x_prompt.shape
    T = x_sample.shape[1]
    P = cache_b_k.shape[2]
    pos_p = jnp.arange(S, dtype=jnp.int32)
    pos_s = P + jnp.arange(T, dtype=jnp.int32)
    yp, ys = x_prompt, x_sample
    st_p, st_s, mk_p, mv_p = [], [], [], []
    for l in range(DEPTH):
        lw = (ln_g[l], ln_b[l], ffn1_w_gu[l], ffn1_w_d[l], ffn2_w_gu[l], ffn2_w_d[l], w_in[l],
              a_rel_bias[l], c_lambda[l], c_subln_g[l], w_branch_a[l], w_branch_b[l], w_branch_c[l],
              w_out[l], w_mem_q[l], w_mem_o[l])
        mk = (mem_prompt @ w_mem_k[l]).reshape(Bp, N_MEM, MEM_HEADS, MEM_HEAD_DIM)
        mv = (mem_prompt @ w_mem_v[l]).reshape(Bp, N_MEM, MEM_HEADS, MEM_HEAD_DIM)
        yp, sp = trunk_layer(yp, pos_p, None, mk, mv, l, *lw)
        past = (cache_a_k[l], cache_a_v[l], cache_b_k[l], cache_b_v[l], cache_b_idx[l],
                cache_c_k[l], cache_c_v[l])
        ys, ss = trunk_layer(ys, pos_s, past, cache_mem_k[l], cache_mem_v[l], l, *lw)
        st_p.append(sp)
        st_s.append(ss)
        mk_p.append(mk)
        mv_p.append(mv)
    (new_a_k_p, new_a_v_p, new_b_k_p, new_b_v_p, new_b_idx_p, new_c_k_p, new_c_v_p) = [
        jnp.stack([s[i] for s in st_p]) for i in range(7)]
    (new_a_k_s, new_a_v_s, new_b_k_s, new_b_v_s, new_b_idx_s, new_c_k_s, new_c_v_s) = [
        jnp.stack([s[i] for s in st_s]) for i in range(7)]
    new_mem_k_p = jnp.stack(mk_p)
    new_mem_v_p = jnp.stack(mv_p)
    return (yp, ys, new_a_k_p, new_a_v_p, new_b_k_p, new_b_v_p, new_b_idx_p, new_c_k_p, new_c_v_p,
            new_mem_k_p, new_mem_v_p, new_a_k_s, new_a_v_s, new_b_k_s, new_b_v_s, new_b_idx_s,
            new_c_k_s, new_c_v_s)
```

```python
import functools
import math

import jax
import jax.numpy as jnp
import numpy as np
from jax import lax
from jax.experimental import pallas as pl
from jax.experimental.pallas import tpu as pltpu

F32 = jnp.float32
BF16 = jnp.bfloat16

D_MODEL = 1024
CHUNK = 64
HEAD_DIM = 64
ROT_DIM = HEAD_DIM // 4
ROT_HALF = ROT_DIM // 2
ROPE_THETA = 500000.0
A_HEADS = 4
A_LEFT_CHUNKS = 8
A_WINDOW = A_LEFT_CHUNKS * CHUNK
REL_CLIP = 128
B_HEADS = 4
IDX_HEADS = 8
IDX_DIM = 64
TOPK_MAX = 256
C_HEADS = 4
N_MEM = 256
MEM_HEADS = 4
MEM_HEAD_DIM = D_MODEL // MEM_HEADS
D_FF = ((8 * D_MODEL // 3 + 127) // 128) * 128
N_BRANCH = 3
DEPTH = 2
ALPHA = (2.0 * DEPTH) ** 0.25
LN_EPS = 1e-5
SUBLN_EPS = 1e-5
A_WIDTH = A_HEADS * HEAD_DIM
B_WIDTH = B_HEADS * HEAD_DIM
C_WIDTH = C_HEADS * 2 * HEAD_DIM
IN_SIZES = (A_WIDTH, A_WIDTH, A_WIDTH,
            B_WIDTH, HEAD_DIM, HEAD_DIM, IDX_HEADS * IDX_DIM, IDX_DIM, IDX_HEADS,
            C_WIDTH, C_WIDTH, C_WIDTH,
            N_BRANCH * D_MODEL)
IN_OFFS = tuple(int(v) for v in np.cumsum((0,) + IN_SIZES))

LANES = 128
V7X_VMEM_LIMIT = 56 * 1024 * 1024

NEG = -1e30
INT_MIN = -2 ** 31

IW_PAD = LANES - HEAD_DIM - IDX_HEADS
P_AQ, P_AK, P_AV = 0, 256, 512
P_B = 768
P_I = 1152
P_C = 1792
P_END = 3328


def _dot(a, b):
    return jnp.dot(a, b, preferred_element_type=F32)


def _dot_nt(a, b):
    return lax.dot_general(a, b, (((1,), (1,)), ((), ())), preferred_element_type=F32)


def _layer_norm(v, g, b):
    mu = jnp.mean(v, axis=-1, keepdims=True)
    d = v - mu
    var = jnp.mean(d * d, axis=-1, keepdims=True)
    return d * lax.rsqrt(var + LN_EPS) * g + b


def _params(sem):
    return pltpu.CompilerParams(dimension_semantics=sem, vmem_limit_bytes=V7X_VMEM_LIMIT)


def _row_tile(n, want):
    t = min(n, want)
    assert n % t == 0
    return t


def _ffn_kernel(x_ref, wg_ref, wu_ref, wd_ref, g_ref, b_ref, o_ref, xb_ref, acc_ref):
    j = pl.program_id(1)

    @pl.when(j == 0)
    def _():
        xb_ref[...] = x_ref[...].astype(BF16)
        acc_ref[...] = jnp.zeros_like(acc_ref)

    xb = xb_ref[...]
    g = _dot(xb, wg_ref[...])
    u = _dot(xb, wu_ref[...])
    h = g * jax.nn.sigmoid(g) * u
    acc_ref[...] += _dot(h.astype(BF16), wd_ref[...])

    @pl.when(j == pl.num_programs(1) - 1)
    def _():
        o_ref[...] = _layer_norm(ALPHA * x_ref[...] + 0.5 * acc_ref[...], g_ref[...], b_ref[...])


def _ffn(x, w_gu, w_d, g, b):
    n = x.shape[0]
    tm = _row_tile(n, 1024)
    tf = 256
    nf = D_FF // tf
    return pl.pallas_call(
        _ffn_kernel,
        out_shape=jax.ShapeDtypeStruct((n, D_MODEL), F32),
        grid=(n // tm, nf),
        in_specs=[
            pl.BlockSpec((tm, D_MODEL), lambda i, j: (i, 0)),
            pl.BlockSpec((D_MODEL, tf), lambda i, j: (0, j)),
            pl.BlockSpec((D_MODEL, tf), lambda i, j: (0, j + nf)),
            pl.BlockSpec((tf, D_MODEL), lambda i, j: (j, 0)),
            pl.BlockSpec((1, D_MODEL), lambda i, j: (0, 0)),
            pl.BlockSpec((1, D_MODEL), lambda i, j: (0, 0)),
        ],
        out_specs=pl.BlockSpec((tm, D_MODEL), lambda i, j: (i, 0)),
        scratch_shapes=[pltpu.VMEM((tm, D_MODEL), BF16), pltpu.VMEM((tm, D_MODEL), F32)],
        compiler_params=_params(("parallel", "arbitrary")),
        name="ffn",
    )(x, w_gu, w_gu, w_d, g, b)


def _rope_tables(pos):
    t = pos.shape[0]
    inv_freq = ROPE_THETA ** (-jnp.arange(ROT_HALF, dtype=F32) / ROT_HALF)
    ang = pos.astype(F32)[:, None] * inv_freq
    cos, sin = jnp.cos(ang), jnp.sin(ang)
    rest = HEAD_DIM - ROT_DIM
    ones = lambda w: jnp.ones((t, w), F32)
    zeros = lambda w: jnp.zeros((t, w), F32)
    c64 = jnp.concatenate([cos, cos, ones(rest)], axis=1)
    lo64 = jnp.concatenate([-sin, zeros(ROT_HALF), zeros(rest)], axis=1)
    hi64 = jnp.concatenate([zeros(ROT_HALF), sin, zeros(rest)], axis=1)
    iw64 = jnp.concatenate([jnp.full((t, IDX_HEADS), IDX_HEADS ** -0.5, F32), ones(HEAD_DIM - IDX_HEADS)], axis=1)
    cat = lambda a, b_: jnp.concatenate([a, b_], axis=1)
    both = jnp.stack([cat(c64, c64), cat(lo64, lo64), cat(hi64, hi64)])
    upper_id = jnp.stack([cat(c64, ones(HEAD_DIM)), cat(lo64, zeros(HEAD_DIM)), cat(hi64, zeros(HEAD_DIM))])
    upper_iw = jnp.stack([cat(c64, iw64), cat(lo64, zeros(HEAD_DIM)), cat(hi64, zeros(HEAD_DIM))])
    return jnp.stack([both, upper_id, upper_iw])


def _rope_apply(z, tab_ref, variant):
    cos = tab_ref[variant, 0]
    lo = tab_ref[variant, 1]
    hi = tab_ref[variant, 2]
    outs = []
    for k in range(z.shape[1] // LANES):
        zc = z[:, k * LANES:(k + 1) * LANES]
        outs.append(zc * cos + pltpu.roll(zc, LANES - ROT_HALF, 1) * lo + pltpu.roll(zc, ROT_HALF, 1) * hi)
    return outs[0] if len(outs) == 1 else jnp.concatenate(outs, axis=1)


def _inproj_kernel(x_ref, w_ref, tab_ref,
                   aq_ref, ak_ref, av_ref, bq_ref, bk_ref, bv_ref,
                   iq_ref, ik_ref, iw_ref, cq_ref, ck_ref, cv_ref):
    xb = x_ref[...].astype(BF16)

    za = _dot(xb, w_ref[:, P_AQ:P_B])
    aq_ref[...] = za[:, 0:256].astype(BF16)
    ak_ref[...] = za[:, 256:512]
    av_ref[...] = za[:, 512:768]

    zb = _dot(xb, w_ref[:, P_B:P_I])
    bq_ref[...] = _rope_apply(zb[:, 0:256], tab_ref, 0).astype(BF16)
    kv = _rope_apply(zb[:, 256:384], tab_ref, 1)
    bk_ref[...] = kv[:, 0:HEAD_DIM]
    bv_ref[...] = kv[:, HEAD_DIM:2 * HEAD_DIM]

    zi = _dot(xb, w_ref[:, P_I:P_C])
    iq_ref[...] = _rope_apply(zi[:, 0:512], tab_ref, 0).astype(BF16)
    kw = _rope_apply(zi[:, 512:640], tab_ref, 2)
    ik_ref[...] = kw[:, 0:IDX_DIM]
    iw_ref[...] = kw[:, IDX_DIM:IDX_DIM + IDX_HEADS]

    zc = _dot(xb, w_ref[:, P_C:P_END])
    cq_ref[...] = _rope_apply(zc[:, 0:512], tab_ref, 0).astype(BF16)
    ck_ref[...] = _rope_apply(zc[:, 512:1024], tab_ref, 0)
    cv_ref[...] = zc[:, 1024:1536]


def _inproj(x, w_pack, tab):
    n = x.shape[0]
    tm = _row_tile(n, 256)
    ntab = tab.shape[2] // tm
    widths = ((256, BF16), (256, F32), (256, F32), (256, BF16), (64, F32), (64, F32),
              (512, BF16), (64, F32), (8, F32), (512, BF16), (512, F32), (512, F32))
    return pl.pallas_call(
        _inproj_kernel,
        out_shape=tuple(jax.ShapeDtypeStruct((n, w), dt) for w, dt in widths),
        grid=(n // tm,),
        in_specs=[
            pl.BlockSpec((tm, D_MODEL), lambda i: (i, 0)),
            pl.BlockSpec((D_MODEL, P_END), lambda i: (0, 0)),
            pl.BlockSpec((3, 3, tm, LANES), lambda i: (0, 0, i % ntab, 0)),
        ],
        out_specs=tuple(pl.BlockSpec((tm, w), lambda i: (i, 0)) for w, _ in widths),
        compiler_params=_params(("parallel",)),
        name="inproj",
    )(x, w_pack, tab)


def _band_bias(rel_bias, q_rows, k_rows, q_off):
    r = jnp.arange(q_rows)[:, None] + q_off
    c = jnp.arange(k_rows)[None, :]
    idx = jnp.clip(r - c, -REL_CLIP, REL_CLIP) + REL_CLIP
    in_band = (c // CHUNK <= r // CHUNK) & (c // CHUNK >= r // CHUNK - A_LEFT_CHUNKS)
    return jnp.where(in_band[None], rel_bias[:, idx].astype(F32), NEG)


def _band_attend(q, pieces, bias_ref, o_ref):
    scale = HEAD_DIM ** -0.5
    for h in range(A_HEADS):
        hs = slice(h * HEAD_DIM, (h + 1) * HEAD_DIM)
        qh = q[:, hs]
        ss = []
        off = 0
        for k, _, extra in pieces:
            n = k.shape[0]
            s = _dot_nt(qh, k[:, hs]) * scale + bias_ref[h, :, off:off + n]
            if extra is not None:
                s = s + extra
            ss.append(s)
            off += n
        m = ss[0].max(axis=-1, keepdims=True)
        for s in ss[1:]:
            m = jnp.maximum(m, s.max(axis=-1, keepdims=True))
        l = jnp.zeros_like(m)
        o = jnp.zeros((q.shape[0], HEAD_DIM), F32)
        for s, (_, v, _) in zip(ss, pieces):
            p = jnp.exp(s - m)
            l = l + p.sum(axis=-1, keepdims=True)
            o = o + _dot(p.astype(BF16), v[:, hs])
        o_ref[0, :, hs] = (o / l).astype(o_ref.dtype)


BAND_QB = 256
BAND_KB = 3


def _band_prompt_kernel(q_ref, k0_ref, k1_ref, k2_ref, v0_ref, v1_ref, v2_ref, bias_ref, o_ref):
    j = pl.program_id(1)
    pieces = []
    for kb, (k_ref, v_ref) in enumerate(((k0_ref, v0_ref), (k1_ref, v1_ref), (k2_ref, v2_ref))):
        back = BAND_KB - 1 - kb
        extra = None if back == 0 else jnp.where(j >= back, 0.0, NEG)
        pieces.append((k_ref[0].astype(BF16), v_ref[0].astype(BF16), extra))
    _band_attend(q_ref[0], pieces, bias_ref, o_ref)


def _band_prompt(aq, ak, av, rel_bias):
    b, s, _ = aq.shape
    assert A_WINDOW == (BAND_KB - 1) * BAND_QB and s % BAND_QB == 0
    bias = _band_bias(rel_bias, BAND_QB, BAND_KB * BAND_QB, A_WINDOW)
    qspec = pl.BlockSpec((1, BAND_QB, A_WIDTH), lambda bi, j: (bi, j, 0))
    kspecs = [pl.BlockSpec((1, BAND_QB, A_WIDTH),
                           functools.partial(lambda bi, j, back: (bi, jnp.maximum(j - back, 0), 0),
                                             back=BAND_KB - 1 - kb))
              for kb in range(BAND_KB)]
    return pl.pallas_call(
        _band_prompt_kernel,
        out_shape=jax.ShapeDtypeStruct((b, s, A_WIDTH), BF16),
        grid=(b, s // BAND_QB),
        in_specs=[qspec] + kspecs + kspecs + [
            pl.BlockSpec((A_HEADS, BAND_QB, BAND_KB * BAND_QB), lambda bi, j: (0, 0, 0))],
        out_specs=qspec,
        compiler_params=_params(("parallel", "parallel")),
        name="band_prompt",
    )(aq, ak, ak, ak, av, av, av, bias)


def _band_step_kernel(q_ref, pk_ref, k_ref, pv_ref, v_ref, bias_ref, o_ref):
    pieces = [(pk_ref[0].astype(BF16), pv_ref[0].astype(BF16), None),
              (k_ref[0].astype(BF16), v_ref[0].astype(BF16), None)]
    _band_attend(q_ref[0], pieces, bias_ref, o_ref)


def _band_step(aq, ak, av, pk, pv, rel_bias):
    b, t, _ = aq.shape
    w = pk.shape[1]
    r = jnp.arange(t)[:, None] + w
    c = jnp.arange(w + t)[None, :]
    bias = rel_bias[:, jnp.clip(r - c, -REL_CLIP, REL_CLIP) + REL_CLIP].astype(F32)
    new = pl.BlockSpec((1, t, A_WIDTH), lambda bi: (bi, 0, 0))
    old = pl.BlockSpec((1, w, A_WIDTH), lambda bi: (bi, 0, 0))
    return pl.pallas_call(
        _band_step_kernel,
        out_shape=jax.ShapeDtypeStruct((b, t, A_WIDTH), BF16),
        grid=(b,),
        in_specs=[new, old, new, old, new, pl.BlockSpec((A_HEADS, t, w + t), lambda bi: (0, 0, 0))],
        out_specs=new,
        compiler_params=_params(("parallel",)),
        name="band_step",
    )(aq, pk, ak, pv, av, bias)


def _admissible(q0, q_rows, k_cols, k_true):
    kpos = lax.broadcasted_iota(jnp.int32, (q_rows, k_cols), 1)
    qpos = lax.broadcasted_iota(jnp.int32, (q_rows, k_cols), 0) + q0
    shift = CHUNK.bit_length() - 1
    return ((kpos >> shift) <= (qpos >> shift)) & (kpos < k_true)


def _select_topk(score, adm, topk):
    q_rows, l = score.shape
    bits = lax.bitcast_convert_type(score, jnp.int32)
    key = bits ^ ((bits >> 31) & jnp.int32(0x7FFFFFFF))
    kf = jnp.float32(topk)

    def body(it, prefix):
        cand = prefix | jnp.left_shift(jnp.int32(1), 31 - it)
        ge = key >= (cand ^ jnp.int32(INT_MIN))
        cnt = jnp.sum(jnp.where(ge, 1.0, 0.0), axis=1, keepdims=True)
        return jnp.where(cnt >= kf, cand, prefix)

    prefix = lax.fori_loop(0, 32, body, jnp.zeros((q_rows, 1), jnp.int32))
    thr = prefix ^ jnp.int32(INT_MIN)
    need = kf - jnp.sum(jnp.where(key > thr, 1.0, 0.0), axis=1, keepdims=True)
    rr = lax.broadcasted_iota(jnp.int32, (LANES, LANES), 0)
    cc = lax.broadcasted_iota(jnp.int32, (LANES, LANES), 1)
    before = jnp.where(rr < cc, 1.0, 0.0).astype(BF16)
    carry = jnp.zeros((q_rows, 1), F32)
    sel = []
    for t in range(l // LANES):
        key_t = key[:, t * LANES:(t + 1) * LANES]
        eq_t = jnp.where(key_t == thr, 1.0, 0.0)
        rank = carry + _dot(eq_t.astype(BF16), before)
        take_tie = jnp.where(rank < need, eq_t, 0.0)
        sel.append(jnp.where(key_t > thr, 1.0, take_tie))
        carry = carry + jnp.sum(eq_t, axis=1, keepdims=True)
    sel = jnp.concatenate(sel, axis=1)
    return jnp.where(adm, sel, 0.0)


def _dsa_kernel(q0_base, k_true, topk, iq_ref, iw_ref, bq_ref, ik_ref, bk_ref, bv_ref, o_ref,
                ikb_ref, bkb_ref, bvb_ref):
    i = pl.program_id(1)
    qb = iq_ref.shape[1]
    l = ik_ref.shape[1]

    @pl.when(i == 0)
    def _():
        ikb_ref[...] = ik_ref[0].astype(BF16)
        bkb_ref[...] = bk_ref[0].astype(BF16)
        bvb_ref[...] = bv_ref[0].astype(BF16)

    iq = iq_ref[0]
    iw = iw_ref[0]
    ikb = ikb_ref[...]
    score = jnp.zeros((qb, l), F32)
    for h in range(IDX_HEADS):
        hs = slice(h * IDX_DIM, (h + 1) * IDX_DIM)
        score = score + iw[:, h:h + 1] * jnp.maximum(_dot_nt(iq[:, hs], ikb), 0.0)
    adm = _admissible(q0_base + i * qb, qb, l, k_true)
    score = jnp.where(adm, score + 0.0, -jnp.inf)
    sel = _select_topk(score, adm, topk)
    mask = jnp.where(sel > 0.0, 0.0, NEG)

    bq = bq_ref[0]
    bkb = bkb_ref[...]
    bvb = bvb_ref[...]
    scale = HEAD_DIM ** -0.5
    for h in range(B_HEADS):
        hs = slice(h * HEAD_DIM, (h + 1) * HEAD_DIM)
        s = _dot_nt(bq[:, hs], bkb) * scale + mask
        m = s.max(axis=-1, keepdims=True)
        p = jnp.exp(s - m)
        den = p.sum(axis=-1, keepdims=True)
        o = _dot(p.astype(BF16), bvb)
        o_ref[0, :, hs] = (o / den).astype(o_ref.dtype)


def _dsa(iq, iw, bq, ik_all, bk_all, bv_all, q0_base, k_true, qb):
    b, t, _ = iq.shape
    l = ik_all.shape[1]
    topk = min(TOPK_MAX, k_true // 4)
    qspec = lambda w: pl.BlockSpec((1, qb, w), lambda bi, i: (bi, i, 0))
    kspec = pl.BlockSpec((1, l, HEAD_DIM), lambda bi, i: (bi, 0, 0))
    return pl.pallas_call(
        functools.partial(_dsa_kernel, q0_base, k_true, topk),
        out_shape=jax.ShapeDtypeStruct((b, t, B_WIDTH), BF16),
        grid=(b, t // qb),
        in_specs=[qspec(IDX_HEADS * IDX_DIM), qspec(IDX_HEADS), qspec(B_WIDTH), kspec, kspec, kspec],
        out_specs=qspec(B_WIDTH),
        scratch_shapes=[pltpu.VMEM((l, HEAD_DIM), BF16)] * 3,
        compiler_params=_params(("parallel", "arbitrary")),
        name="dsa",
    )(iq, iw, bq, ik_all, bk_all, bv_all)


def _diff_kernel(q0_base, k_true, lam_init, q_ref, k_ref, v_ref, lam_ref, g_ref, o_ref, kb_ref, vb_ref):
    i = pl.program_id(1)
    qb = q_ref.shape[1]
    l = k_ref.shape[1]

    @pl.when(i == 0)
    def _():
        kb_ref[...] = k_ref[0].astype(BF16)
        vb_ref[...] = v_ref[0].astype(BF16)

    lp = lam_ref[...]
    lam = (jnp.exp(jnp.sum(lp[0:1] * lp[1:2], axis=1, keepdims=True))
           - jnp.exp(jnp.sum(lp[2:3] * lp[3:4], axis=1, keepdims=True)) + lam_init)
    adm = _admissible(q0_base + i * qb, qb, l, k_true)
    mask = jnp.where(adm, 0.0, NEG)
    q = q_ref[0]
    scale = HEAD_DIM ** -0.5
    gain = g_ref[...] * (1.0 - lam_init)
    for h in range(C_HEADS):
        probs = []
        for part in range(2):
            cs = slice((2 * h + part) * HEAD_DIM, (2 * h + part + 1) * HEAD_DIM)
            s = _dot_nt(q[:, cs], kb_ref[:, cs]) * scale + mask
            m = s.max(axis=-1, keepdims=True)
            p = jnp.exp(s - m)
            probs.append(p * (1.0 / p.sum(axis=-1, keepdims=True)))
        a = probs[0] - lam * probs[1]
        vs = slice(h * 2 * HEAD_DIM, (h + 1) * 2 * HEAD_DIM)
        o = _dot(a.astype(BF16), vb_ref[:, vs])
        o = o * lax.rsqrt(jnp.mean(o * o, axis=-1, keepdims=True) + SUBLN_EPS) * gain
        o_ref[0, :, vs] = o.astype(o_ref.dtype)


def _diff(cq, ck_all, cv_all, lam_p, subln_g, layer, q0_base, k_true, qb):
    b, t, _ = cq.shape
    l = ck_all.shape[1]
    lam_init = 0.8 - 0.6 * math.exp(-0.3 * layer)
    qspec = pl.BlockSpec((1, qb, C_WIDTH), lambda bi, i: (bi, i, 0))
    kspec = pl.BlockSpec((1, l, C_WIDTH), lambda bi, i: (bi, 0, 0))
    return pl.pallas_call(
        functools.partial(_diff_kernel, q0_base, k_true, lam_init),
        out_shape=jax.ShapeDtypeStruct((b, t, C_WIDTH), BF16),
        grid=(b, t // qb),
        in_specs=[qspec, kspec, kspec,
                  pl.BlockSpec((4, HEAD_DIM), lambda bi, i: (0, 0)),
                  pl.BlockSpec((1, 2 * HEAD_DIM), lambda bi, i: (0, 0))],
        out_specs=qspec,
        scratch_shapes=[pltpu.VMEM((l, C_WIDTH), BF16)] * 2,
        compiler_params=_params(("parallel", "arbitrary")),
        name="diff",
    )(cq, ck_all, cv_all, lam_p, subln_g)


def _merge_kernel(x_ref, ya_ref, yb_ref, yc_ref, wg_ref, wa_ref, wb_ref, wc_ref, wo_ref, g_ref, b_ref, o_ref):
    x = x_ref[...]
    xb = x.astype(BF16)
    merged = None
    for k, (y_ref, w_ref) in enumerate(((ya_ref, wa_ref), (yb_ref, wb_ref), (yc_ref, wc_ref))):
        gate = jax.nn.sigmoid(_dot(xb, wg_ref[:, k * D_MODEL:(k + 1) * D_MODEL]))
        term = gate * _dot(y_ref[...], w_ref[...])
        merged = term if merged is None else merged + term
    out = _dot(merged.astype(BF16), wo_ref[...])
    o_ref[...] = _layer_norm(ALPHA * x + out, g_ref[...], b_ref[...])


def _merge(x, ya, yb, yc, w_gates, wb_a, wb_b, wb_c, w_out, g, b):
    n = x.shape[0]
    tm = _row_tile(n, 512)
    row = lambda w: pl.BlockSpec((tm, w), lambda i: (i, 0))
    full = lambda a: pl.BlockSpec(a.shape, lambda i: (0, 0))
    return pl.pallas_call(
        _merge_kernel,
        out_shape=jax.ShapeDtypeStruct((n, D_MODEL), F32),
        grid=(n // tm,),
        in_specs=[row(D_MODEL), row(A_WIDTH), row(B_WIDTH), row(C_WIDTH),
                  full(w_gates), full(wb_a), full(wb_b), full(wb_c), full(w_out), full(g), full(b)],
        out_specs=row(D_MODEL),
        compiler_params=_params(("parallel",)),
        name="merge",
    )(x, ya, yb, yc, w_gates, wb_a, wb_b, wb_c, w_out, g, b)


def _mem_kernel(x_ref, mk_ref, mv_ref, wq_ref, wo_ref, g_ref, b_ref, o_ref, mkb_ref, mvb_ref):
    i = pl.program_id(1)

    @pl.when(i == 0)
    def _():
        mkb_ref[...] = mk_ref[0].astype(BF16)
        mvb_ref[...] = mv_ref[0].astype(BF16)

    x = x_ref[0]
    q = _dot(x.astype(BF16), wq_ref[...]).astype(BF16)
    scale = MEM_HEAD_DIM ** -0.5
    heads = []
    for h in range(MEM_HEADS):
        hs = slice(h * MEM_HEAD_DIM, (h + 1) * MEM_HEAD_DIM)
        s = _dot_nt(q[:, hs], mkb_ref[:, hs]) * scale
        m = s.max(axis=-1, keepdims=True)
        p = jnp.exp(s - m)
        den = p.sum(axis=-1, keepdims=True)
        heads.append((_dot(p.astype(BF16), mvb_ref[:, hs]) / den).astype(BF16))
    o = _dot(jnp.concatenate(heads, axis=1), wo_ref[...])
    o_ref[0] = _layer_norm(ALPHA * x + o, g_ref[...], b_ref[...])


def _mem_attn(x, mem_k, mem_v, w_q, w_o, g, b):
    bsz, t, _ = x.shape
    tq = _row_tile(t, 512)
    xspec = pl.BlockSpec((1, tq, D_MODEL), lambda bi, i: (bi, i, 0))
    mspec = pl.BlockSpec((1, N_MEM, D_MODEL), lambda bi, i: (bi, 0, 0))
    full = lambda a: pl.BlockSpec(a.shape, lambda bi, i: (0, 0))
    return pl.pallas_call(
        _mem_kernel,
        out_shape=jax.ShapeDtypeStruct((bsz, t, D_MODEL), F32),
        grid=(bsz, t // tq),
        in_specs=[xspec, mspec, mspec, full(w_q), full(w_o), full(g), full(b)],
        out_specs=xspec,
        scratch_shapes=[pltpu.VMEM((N_MEM, D_MODEL), BF16)] * 2,
        compiler_params=_params(("parallel", "arbitrary")),
        name="mem_attn",
    )(x, mem_k, mem_v, w_q, w_o, g, b)


def _memproj_kernel(x_ref, wk_ref, wv_ref, k_ref, v_ref):
    xb = x_ref[...].astype(BF16)
    k_ref[...] = _dot(xb, wk_ref[...])
    v_ref[...] = _dot(xb, wv_ref[...])


def _memproj(mem, w_k, w_v):
    n = mem.shape[0]
    tm = _row_tile(n, 512)
    row = pl.BlockSpec((tm, D_MODEL), lambda i: (i, 0))
    full = pl.BlockSpec((D_MODEL, D_MODEL), lambda i: (0, 0))
    return pl.pallas_call(
        _memproj_kernel,
        out_shape=(jax.ShapeDtypeStruct((n, D_MODEL), F32),) * 2,
        grid=(n // tm,),
        in_specs=[row, full, full],
        out_specs=(row, row),
        compiler_params=_params(("parallel",)),
        name="memproj",
    )(mem, w_k, w_v)


def _pad_rows(a, rows):
    pad = rows - a.shape[1]
    if pad == 0:
        return a
    return jnp.pad(a, ((0, 0), (0, pad)) + ((0, 0),) * (a.ndim - 2))


def _trunk_layer(x, pos0, past, mem_k, mem_v, layer, w):
    b, t, _ = x.shape
    n = b * t
    x2 = _ffn(x.reshape(n, D_MODEL), w["f1_gu"], w["f1_d"], w["ln_g"][0:1], w["ln_b"][0:1])

    tm = _row_tile(n, 256)
    pos = pos0 + jnp.arange(t, dtype=jnp.int32)
    tab = _rope_tables(pos)
    if t < tm:
        tab = jnp.tile(tab, (1, 1, tm // t, 1))
    outs = _inproj(x2, w["in_pack"], tab)
    aq, ak, av, bq, bk, bv, iq, ik, iw, cq, ck, cv = [o.reshape(b, t, o.shape[-1]) for o in outs]

    if past is None:
        ya = _band_prompt(aq, ak, av, w["rel_bias"])
        keep = min(A_WINDOW, t)
        a_new_k, a_new_v = ak[:, t - keep:], av[:, t - keep:]
        k_true = t
        ik_all, bk_all, bv_all, ck_all, cv_all = ik, bk, bv, ck, cv
        qb = 128
    else:
        pa_k, pa_v, pb_k, pb_v, pb_i, pc_k, pc_v = past
        p_len = pb_k.shape[1]
        ya = _band_step(aq, ak, av, pa_k.reshape(b, -1, A_WIDTH), pa_v.reshape(b, -1, A_WIDTH), w["rel_bias"])
        a_new_k, a_new_v = ak, av
        k_true = p_len + t
        l_pad = -(-k_true // LANES) * LANES
        cat = lambda old, new: _pad_rows(jnp.concatenate([old.reshape(b, p_len, -1), new], axis=1), l_pad)
        ik_all, bk_all, bv_all = cat(pb_i, ik), cat(pb_k, bk), cat(pb_v, bv)
        ck_all, cv_all = cat(pc_k, ck), cat(pc_v, cv)
        qb = t
    yb = _dsa(iq, iw, bq, ik_all, bk_all, bv_all, pos0, k_true, qb)
    yc = _diff(cq, ck_all, cv_all, w["lam"], w["subln_g"], layer, pos0, k_true, qb)

    x3 = _merge(x2, ya.reshape(n, -1), yb.reshape(n, -1), yc.reshape(n, -1),
                w["gates"], w["wb_a"], w["wb_b"], w["wb_c"], w["w_out"], w["ln_g"][1:2], w["ln_b"][1:2])
    x4 = _mem_attn(x3.reshape(b, t, D_MODEL), mem_k, mem_v, w["wm_q"], w["wm_o"], w["ln_g"][2:3], w["ln_b"][2:3])
    x5 = _ffn(x4.reshape(n, D_MODEL), w["f2_gu"], w["f2_d"], w["ln_g"][3:4], w["ln_b"][3:4])
    state = (a_new_k.reshape(b, -1, A_HEADS, HEAD_DIM), a_new_v.reshape(b, -1, A_HEADS, HEAD_DIM),
             bk, bv, ik,
             ck.reshape(b, t, C_HEADS, 2 * HEAD_DIM), cv.reshape(b, t, C_HEADS, 2 * HEAD_DIM))
    return x5.reshape(b, t, D_MODEL), state


def _layer_weights(l, ln_g, ln_b, ffn1_w_gu, ffn1_w_d, ffn2_w_gu, ffn2_w_d, w_in, a_rel_bias, c_lambda,
                   c_subln_g, w_branch_a, w_branch_b, w_branch_c, w_out, w_mem_q, w_mem_k, w_mem_v, w_mem_o):
    wi = w_in[l]
    iw_end = IN_OFFS[9]
    gates_at = IN_OFFS[12]
    in_pack = jnp.concatenate(
        [wi[:, :iw_end], jnp.zeros((D_MODEL, IW_PAD), wi.dtype), wi[:, iw_end:gates_at]], axis=1)
    bf = lambda a: a.astype(BF16)
    return dict(
        ln_g=ln_g[l].astype(F32), ln_b=ln_b[l].astype(F32),
        f1_gu=bf(ffn1_w_gu[l]), f1_d=bf(ffn1_w_d[l]), f2_gu=bf(ffn2_w_gu[l]), f2_d=bf(ffn2_w_d[l]),
        in_pack=bf(in_pack), gates=bf(wi[:, gates_at:]),
        rel_bias=a_rel_bias[l], lam=c_lambda[l].astype(F32), subln_g=c_subln_g[l].astype(F32)[None, :],
        wb_a=bf(w_branch_a[l]), wb_b=bf(w_branch_b[l]), wb_c=bf(w_branch_c[l]), w_out=bf(w_out[l]),
        wm_q=bf(w_mem_q[l]), wm_k=bf(w_mem_k[l]), wm_v=bf(w_mem_v[l]), wm_o=bf(w_mem_o[l]),
    )


def kernel(x_prompt, x_sample, cache_a_k, cache_a_v, cache_b_k, cache_b_v, cache_b_idx, cache_c_k, cache_c_v, cache_mem_k, cache_mem_v, mem_prompt, ln_g, ln_b, ffn1_w_gu, ffn1_w_d, ffn2_w_gu, ffn2_w_d, w_in, a_rel_bias, c_lambda, c_subln_g, w_branch_a, w_branch_b, w_branch_c, w_out, w_mem_q, w_mem_k, w_mem_v, w_mem_o):
    bp, s, _ = x_prompt.shape
    bs, t, _ = x_sample.shape
    p_len = cache_b_k.shape[2]
    depth = ln_g.shape[0]
    yp, ys = x_prompt, x_sample
    st_p, st_s, mk_p, mv_p = [], [], [], []
    for l in range(depth):
        w = _layer_weights(l, ln_g, ln_b, ffn1_w_gu, ffn1_w_d, ffn2_w_gu, ffn2_w_d, w_in, a_rel_bias,
                           c_lambda, c_subln_g, w_branch_a, w_branch_b, w_branch_c, w_out,
                           w_mem_q, w_mem_k, w_mem_v, w_mem_o)
        mk, mv = _memproj(mem_prompt.reshape(bp * N_MEM, D_MODEL), w["wm_k"], w["wm_v"])
        mk = mk.reshape(bp, N_MEM, D_MODEL)
        mv = mv.reshape(bp, N_MEM, D_MODEL)
        yp, sp = _trunk_layer(yp, 0, None, mk, mv, l, w)
        past = (cache_a_k[l], cache_a_v[l], cache_b_k[l], cache_b_v[l], cache_b_idx[l],
                cache_c_k[l], cache_c_v[l])
        ys, ss = _trunk_layer(ys, p_len, past, cache_mem_k[l].reshape(bs, N_MEM, D_MODEL),
                              cache_mem_v[l].reshape(bs, N_MEM, D_MODEL), l, w)
        st_p.append(sp)
        st_s.append(ss)
        mk_p.append(mk.reshape(bp, N_MEM, MEM_HEADS, MEM_HEAD_DIM))
        mv_p.append(mv.reshape(bp, N_MEM, MEM_HEADS, MEM_HEAD_DIM))
    stk = lambda sts: [jnp.stack([st[i] for st in sts]) for i in range(7)]
    (a_k_p, a_v_p, b_k_p, b_v_p, b_i_p, c_k_p, c_v_p) = stk(st_p)
    (a_k_s, a_v_s, b_k_s, b_v_s, b_i_s, c_k_s, c_v_s) = stk(st_s)
    return (yp, ys, a_k_p, a_v_p, b_k_p, b_v_p, b_i_p, c_k_p, c_v_p,
            jnp.stack(mk_p), jnp.stack(mv_p), a_k_s, a_v_s, b_k_s, b_v_s, b_i_s, c_k_s, c_v_s)
```

```python
import functools
import math

import jax
import jax.numpy as jnp
import numpy as np
from jax import lax
from jax.experimental import pallas as pl
from jax.experimental.pallas import tpu as pltpu

F32 = jnp.float32
BF16 = jnp.bfloat16

D_MODEL = 1024
CHUNK = 64
HEAD_DIM = 64
ROT_DIM = HEAD_DIM // 4
ROT_HALF = ROT_DIM // 2
ROPE_THETA = 500000.0
A_HEADS = 4
A_LEFT_CHUNKS = 8
A_WINDOW = A_LEFT_CHUNKS * CHUNK
REL_CLIP = 128
B_HEADS = 4
IDX_HEADS = 8
IDX_DIM = 64
TOPK_MAX = 256
C_HEADS = 4
N_MEM = 256
MEM_HEADS = 4
MEM_HEAD_DIM = D_MODEL // MEM_HEADS
D_FF = ((8 * D_MODEL // 3 + 127) // 128) * 128
N_BRANCH = 3
DEPTH = 2
ALPHA = (2.0 * DEPTH) ** 0.25
LN_EPS = 1e-5
SUBLN_EPS = 1e-5
A_WIDTH = A_HEADS * HEAD_DIM
B_WIDTH = B_HEADS * HEAD_DIM
C_WIDTH = C_HEADS * 2 * HEAD_DIM
IN_SIZES = (A_WIDTH, A_WIDTH, A_WIDTH,
            B_WIDTH, HEAD_DIM, HEAD_DIM, IDX_HEADS * IDX_DIM, IDX_DIM, IDX_HEADS,
            C_WIDTH, C_WIDTH, C_WIDTH,
            N_BRANCH * D_MODEL)
IN_OFFS = tuple(int(v) for v in np.cumsum((0,) + IN_SIZES))

LANES = 128
V7X_VMEM_LIMIT = 56 * 1024 * 1024

NEG = -1e30
INT_MIN = -2 ** 31

IW_PAD = LANES - HEAD_DIM - IDX_HEADS
P_AQ, P_AK, P_AV = 0, 256, 512
P_B = 768
P_I = 1152
P_C = 1792
P_END = 3328


def _dot(a, b):
    return jnp.dot(a, b, preferred_element_type=F32)


def _dot_nt(a, b):
    return lax.dot_general(a, b, (((1,), (1,)), ((), ())), preferred_element_type=F32)


def _layer_norm(v, g, b):
    mu = jnp.mean(v, axis=-1, keepdims=True)
    d = v - mu
    var = jnp.mean(d * d, axis=-1, keepdims=True)
    return d * lax.rsqrt(var + LN_EPS) * g + b


def _params(sem):
    return pltpu.CompilerParams(dimension_semantics=sem, vmem_limit_bytes=V7X_VMEM_LIMIT)


def _row_tile(n, want):
    t = min(n, want)
    assert n % t == 0
    return t


def _ffn_kernel(x_ref, wg_ref, wu_ref, wd_ref, g_ref, b_ref, o_ref, xb_ref, acc_ref):
    j = pl.program_id(1)

    @pl.when(j == 0)
    def _():
        xb_ref[...] = x_ref[...].astype(BF16)
        acc_ref[...] = jnp.zeros_like(acc_ref)

    xb = xb_ref[...]
    g = _dot(xb, wg_ref[...])
    u = _dot(xb, wu_ref[...])
    h = g * jax.nn.sigmoid(g) * u
    acc_ref[...] += _dot(h.astype(BF16), wd_ref[...])

    @pl.when(j == pl.num_programs(1) - 1)
    def _():
        o_ref[...] = _layer_norm(ALPHA * x_ref[...] + 0.5 * acc_ref[...], g_ref[...], b_ref[...])


def _ffn(x, w_gu, w_d, g, b):
    n = x.shape[0]
    tm = _row_tile(n, 1024)
    tf = 256
    nf = D_FF // tf
    return pl.pallas_call(
        _ffn_kernel,
        out_shape=jax.ShapeDtypeStruct((n, D_MODEL), F32),
        grid=(n // tm, nf),
        in_specs=[
            pl.BlockSpec((tm, D_MODEL), lambda i, j: (i, 0)),
            pl.BlockSpec((D_MODEL, tf), lambda i, j: (0, j)),
            pl.BlockSpec((D_MODEL, tf), lambda i, j: (0, j + nf)),
            pl.BlockSpec((tf, D_MODEL), lambda i, j: (j, 0)),
            pl.BlockSpec((1, D_MODEL), lambda i, j: (0, 0)),
            pl.BlockSpec((1, D_MODEL), lambda i, j: (0, 0)),
        ],
        out_specs=pl.BlockSpec((tm, D_MODEL), lambda i, j: (i, 0)),
        scratch_shapes=[pltpu.VMEM((tm, D_MODEL), BF16), pltpu.VMEM((tm, D_MODEL), F32)],
        compiler_params=_params(("parallel", "arbitrary")),
        name="ffn",
    )(x, w_gu, w_gu, w_d, g, b)


def _rope_tables(pos):
    t = pos.shape[0]
    inv_freq = ROPE_THETA ** (-jnp.arange(ROT_HALF, dtype=F32) / ROT_HALF)
    ang = pos.astype(F32)[:, None] * inv_freq
    cos, sin = jnp.cos(ang), jnp.sin(ang)
    rest = HEAD_DIM - ROT_DIM
    ones = lambda w: jnp.ones((t, w), F32)
    zeros = lambda w: jnp.zeros((t, w), F32)
    c64 = jnp.concatenate([cos, cos, ones(rest)], axis=1)
    lo64 = jnp.concatenate([-sin, zeros(ROT_HALF), zeros(rest)], axis=1)
    hi64 = jnp.concatenate([zeros(ROT_HALF), sin, zeros(rest)], axis=1)
    iw64 = jnp.concatenate([jnp.full((t, IDX_HEADS), IDX_HEADS ** -0.5, F32), ones(HEAD_DIM - IDX_HEADS)], axis=1)
    cat = lambda a, b_: jnp.concatenate([a, b_], axis=1)
    both = jnp.stack([cat(c64, c64), cat(lo64, lo64), cat(hi64, hi64)])
    upper_id = jnp.stack([cat(c64, ones(HEAD_DIM)), cat(lo64, zeros(HEAD_DIM)), cat(hi64, zeros(HEAD_DIM))])
    upper_iw = jnp.stack([cat(c64, iw64), cat(lo64, zeros(HEAD_DIM)), cat(hi64, zeros(HEAD_DIM))])
    return jnp.stack([both, upper_id, upper_iw])


def _rope_apply(z, tab_ref, variant):
    cos = tab_ref[variant, 0]
    lo = tab_ref[variant, 1]
    hi = tab_ref[variant, 2]
    outs = []
    for k in range(z.shape[1] // LANES):
        zc = z[:, k * LANES:(k + 1) * LANES]
        outs.append(zc * cos + pltpu.roll(zc, LANES - ROT_HALF, 1) * lo + pltpu.roll(zc, ROT_HALF, 1) * hi)
    return outs[0] if len(outs) == 1 else jnp.concatenate(outs, axis=1)


def _inproj_sections(x_ref, w_ref, tab_ref):
    xb = x_ref[...].astype(BF16)
    za = _dot(xb, w_ref[:, P_AQ:P_B])
    zb = _dot(xb, w_ref[:, P_B:P_I])
    zi = _dot(xb, w_ref[:, P_I:P_C])
    zc = _dot(xb, w_ref[:, P_C:P_END])
    return dict(
        aq=za[:, 0:256], ak=za[:, 256:512], av=za[:, 512:768],
        bq=_rope_apply(zb[:, 0:256], tab_ref, 0),
        bkv=_rope_apply(zb[:, 256:384], tab_ref, 1),
        iq=_rope_apply(zi[:, 0:512], tab_ref, 0),
        ikw=_rope_apply(zi[:, 512:640], tab_ref, 2),
        cq=_rope_apply(zc[:, 0:512], tab_ref, 0),
        ck=_rope_apply(zc[:, 512:1024], tab_ref, 0),
        cv=zc[:, 1024:1536])


def _store_heads_interleaved(o_ref, v, heads):
    rows = v.shape[0]
    for h in range(heads):
        o_ref[pl.ds(h, rows, stride=heads), :] = v[:, h * LANES:(h + 1) * LANES]


def _inproj_prompt_kernel(x_ref, w_ref, tab_ref,
                          aq_ref, akt_ref, avt_ref, aktb_ref, avtb_ref,
                          bq_ref, bkt_ref, bvt_ref, ikt_ref, bktb_ref, bvtb_ref, iktb_ref,
                          iq_ref, iw_ref, cq_ref, ck4_ref, cv4_ref, cktb_ref, cvb_ref):
    s = _inproj_sections(x_ref, w_ref, tab_ref)
    aq_ref[...] = s["aq"].astype(BF16)
    akt = s["ak"].T
    avt = s["av"].T
    akt_ref[0] = akt
    avt_ref[0] = avt
    aktb_ref[0] = akt.astype(BF16)
    avtb_ref[0] = avt.astype(BF16)
    bq_ref[...] = s["bq"].astype(BF16)
    kvt = s["bkv"].T
    bkt_ref[0] = kvt[0:HEAD_DIM]
    bvt_ref[0] = kvt[HEAD_DIM:2 * HEAD_DIM]
    bktb_ref[0] = kvt[0:HEAD_DIM].astype(BF16)
    bvtb_ref[0] = kvt[HEAD_DIM:2 * HEAD_DIM].astype(BF16)
    ikt = s["ikw"].T[0:IDX_DIM]
    ikt_ref[0] = ikt
    iktb_ref[0] = ikt.astype(BF16)
    iq_ref[...] = s["iq"].astype(BF16)
    iw_ref[...] = s["ikw"][:, IDX_DIM:IDX_DIM + IDX_HEADS]
    cq_ref[...] = s["cq"].astype(BF16)
    _store_heads_interleaved(ck4_ref, s["ck"], C_HEADS)
    _store_heads_interleaved(cv4_ref, s["cv"], C_HEADS)
    cktb_ref[0] = s["ck"].T.astype(BF16)
    cvb_ref[...] = s["cv"].astype(BF16)


INPROJ_TM = 256


def _inproj_prompt(x, w_pack, tab, bsz, seq):
    n = x.shape[0]
    tm = INPROJ_TM
    spb = seq // tm
    tok = lambda w: pl.BlockSpec((tm, w), lambda i: (i, 0))
    tr = lambda d: pl.BlockSpec((1, d, tm), lambda i: (i // spb, 0, i % spb))
    sds = jax.ShapeDtypeStruct
    tok_o = lambda w, dt: (sds((n, w), dt), tok(w))
    tr_o = lambda d, dt: (sds((bsz, d, seq), dt), tr(d))
    il_o = (sds((n * C_HEADS, LANES), F32), pl.BlockSpec((tm * C_HEADS, LANES), lambda i: (i, 0)))
    outs = [tok_o(A_WIDTH, BF16), tr_o(A_WIDTH, F32), tr_o(A_WIDTH, F32), tr_o(A_WIDTH, BF16), tr_o(A_WIDTH, BF16),
            tok_o(B_WIDTH, BF16), tr_o(HEAD_DIM, F32), tr_o(HEAD_DIM, F32), tr_o(IDX_DIM, F32),
            tr_o(HEAD_DIM, BF16), tr_o(HEAD_DIM, BF16), tr_o(IDX_DIM, BF16),
            tok_o(IDX_HEADS * IDX_DIM, BF16), tok_o(IDX_HEADS, F32), tok_o(C_WIDTH, BF16),
            il_o, il_o, tr_o(C_WIDTH, BF16), tok_o(C_WIDTH, BF16)]
    return pl.pallas_call(
        _inproj_prompt_kernel,
        out_shape=tuple(o[0] for o in outs),
        grid=(n // tm,),
        in_specs=[tok(D_MODEL),
                  pl.BlockSpec((D_MODEL, P_END), lambda i: (0, 0)),
                  pl.BlockSpec((3, 3, tm, LANES), lambda i: (0, 0, i % spb, 0))],
        out_specs=tuple(o[1] for o in outs),
        compiler_params=_params(("parallel",)),
        name="inproj_prompt",
    )(x, w_pack, tab)


def _inproj_step_kernel(x_ref, w_ref, tab_ref,
                        aq_ref, ak_ref, av_ref, bq_ref, bk_ref, bv_ref,
                        iq_ref, ik_ref, iw_ref, cq_ref, ck_ref, cv_ref):
    s = _inproj_sections(x_ref, w_ref, tab_ref)
    aq_ref[...] = s["aq"].astype(BF16)
    ak_ref[...] = s["ak"]
    av_ref[...] = s["av"]
    bq_ref[...] = s["bq"].astype(BF16)
    bk_ref[...] = s["bkv"][:, 0:HEAD_DIM]
    bv_ref[...] = s["bkv"][:, HEAD_DIM:2 * HEAD_DIM]
    iq_ref[...] = s["iq"].astype(BF16)
    ik_ref[...] = s["ikw"][:, 0:IDX_DIM]
    iw_ref[...] = s["ikw"][:, IDX_DIM:IDX_DIM + IDX_HEADS]
    cq_ref[...] = s["cq"].astype(BF16)
    ck_ref[...] = s["ck"]
    cv_ref[...] = s["cv"]


def _inproj_step(x, w_pack, tab):
    n = x.shape[0]
    tm = _row_tile(n, INPROJ_TM)
    widths = ((A_WIDTH, BF16), (A_WIDTH, F32), (A_WIDTH, F32), (B_WIDTH, BF16), (HEAD_DIM, F32), (HEAD_DIM, F32),
              (IDX_HEADS * IDX_DIM, BF16), (IDX_DIM, F32), (IDX_HEADS, F32),
              (C_WIDTH, BF16), (C_WIDTH, F32), (C_WIDTH, F32))
    return pl.pallas_call(
        _inproj_step_kernel,
        out_shape=tuple(jax.ShapeDtypeStruct((n, w), dt) for w, dt in widths),
        grid=(n // tm,),
        in_specs=[
            pl.BlockSpec((tm, D_MODEL), lambda i: (i, 0)),
            pl.BlockSpec((D_MODEL, P_END), lambda i: (0, 0)),
            pl.BlockSpec((3, 3, tm, LANES), lambda i: (0, 0, 0, 0)),
        ],
        out_specs=tuple(pl.BlockSpec((tm, w), lambda i: (i, 0)) for w, _ in widths),
        compiler_params=_params(("parallel",)),
        name="inproj_step",
    )(x, w_pack, tab)


def _qk(q, piece):
    kind, k = piece
    return _dot(q, k) if kind == "T" else _dot_nt(q, k)


def _pv(p, piece):
    kind, v = piece
    return _dot_nt(p, v) if kind == "T" else _dot(p, v)


def _piece_len(piece):
    kind, a = piece
    return a.shape[1] if kind == "T" else a.shape[0]


def _cat(parts):
    return parts[0] if len(parts) == 1 else jnp.concatenate(parts, axis=1)


def _pv_pieces(p, pieces):
    off = 0
    o = None
    for piece in pieces:
        n = _piece_len(piece)
        t = _pv(p[:, off:off + n], piece)
        o = t if o is None else o + t
        off += n
    return o


def _toeplitz_bias(rel_bias, q_rows, k_cols, q_off):
    period = pl.next_power_of_2(q_rows + k_cols)
    e = np.arange(period)
    e = np.where(e >= k_cols, e - period, e)
    idx = np.clip(q_off - e, -REL_CLIP, REL_CLIP) + REL_CLIP
    table = rel_bias[:, idx].astype(F32)
    flat = jnp.tile(table, (1, q_rows + 1))[:, :q_rows * (period - 1)]
    return flat.reshape(rel_bias.shape[0], q_rows, period - 1)[:, :, :k_cols]


def _band_attend(q, pieces, bias_ref, o_ref):
    scale = HEAD_DIM ** -0.5
    for h in range(A_HEADS):
        hs = slice(h * HEAD_DIM, (h + 1) * HEAD_DIM)
        qh = q[:, hs]
        ss = []
        off = 0
        for k_of, _, extra in pieces:
            kp = k_of(h)
            n = _piece_len(kp)
            s = _qk(qh, kp) * scale + bias_ref[h, :, off:off + n]
            if extra is not None:
                s = s + extra
            ss.append(s)
            off += n
        m = ss[0].max(axis=-1, keepdims=True)
        for s in ss[1:]:
            m = jnp.maximum(m, s.max(axis=-1, keepdims=True))
        l = jnp.zeros_like(m)
        o = jnp.zeros((q.shape[0], HEAD_DIM), F32)
        for s, (_, v_of, _) in zip(ss, pieces):
            p = jnp.exp(s - m)
            l = l + p.sum(axis=-1, keepdims=True)
            o = o + _pv(p.astype(BF16), v_of(h))
        o_ref[0, :, hs] = (o / l).astype(o_ref.dtype)


BAND_QB = 256
BAND_KB = 3


def _band_prompt_kernel(q_ref, k0_ref, k1_ref, k2_ref, v0_ref, v1_ref, v2_ref, bias_ref, o_ref):
    j = pl.program_id(1)
    pieces = []
    for kb, (k_ref, v_ref) in enumerate(((k0_ref, v0_ref), (k1_ref, v1_ref), (k2_ref, v2_ref))):
        back = BAND_KB - 1 - kb
        extra = None if back == 0 else jnp.where(j >= back, 0.0, NEG)
        head_rows = lambda ref: (lambda h: ("T", ref[0, h * HEAD_DIM:(h + 1) * HEAD_DIM, :]))
        pieces.append((head_rows(k_ref), head_rows(v_ref), extra))
    _band_attend(q_ref[0], pieces, bias_ref, o_ref)


def _band_prompt(aq, akt, avt, rel_bias):
    b, s, _ = aq.shape
    assert A_WINDOW == (BAND_KB - 1) * BAND_QB and s % BAND_QB == 0
    kw = BAND_KB * BAND_QB
    r = np.arange(BAND_QB)[:, None] + A_WINDOW
    c = np.arange(kw)[None, :]
    in_band = (c // CHUNK <= r // CHUNK) & (c // CHUNK >= r // CHUNK - A_LEFT_CHUNKS)
    bias = jnp.where(in_band[None], _toeplitz_bias(rel_bias, BAND_QB, kw, A_WINDOW), NEG)
    qspec = pl.BlockSpec((1, BAND_QB, A_WIDTH), lambda bi, j: (bi, j, 0))
    kspecs = [pl.BlockSpec((1, A_WIDTH, BAND_QB),
                           functools.partial(lambda bi, j, back: (bi, 0, jnp.maximum(j - back, 0)),
                                             back=BAND_KB - 1 - kb))
              for kb in range(BAND_KB)]
    return pl.pallas_call(
        _band_prompt_kernel,
        out_shape=jax.ShapeDtypeStruct((b, s, A_WIDTH), BF16),
        grid=(b, s // BAND_QB),
        in_specs=[qspec] + kspecs + kspecs + [pl.BlockSpec((A_HEADS, BAND_QB, kw), lambda bi, j: (0, 0, 0))],
        out_specs=qspec,
        compiler_params=_params(("parallel", "parallel")),
        name="band_prompt",
    )(aq, akt, akt, akt, avt, avt, avt, bias)


def _band_step_kernel(q_ref, pk_ref, k_ref, pv_ref, v_ref, bias_ref, o_ref):
    head_cols = lambda a: (lambda h: ("N", a[:, h * HEAD_DIM:(h + 1) * HEAD_DIM]))
    pieces = [(head_cols(pk_ref[0].astype(BF16)), head_cols(pv_ref[0].astype(BF16)), None),
              (head_cols(k_ref[0].astype(BF16)), head_cols(v_ref[0].astype(BF16)), None)]
    _band_attend(q_ref[0], pieces, bias_ref, o_ref)


def _band_step(aq, ak, av, pk, pv, rel_bias):
    b, t, _ = aq.shape
    w = pk.shape[1]
    bias = _toeplitz_bias(rel_bias, t, w + t, w)
    new = pl.BlockSpec((1, t, A_WIDTH), lambda bi: (bi, 0, 0))
    old = pl.BlockSpec((1, w, A_WIDTH), lambda bi: (bi, 0, 0))
    return pl.pallas_call(
        _band_step_kernel,
        out_shape=jax.ShapeDtypeStruct((b, t, A_WIDTH), BF16),
        grid=(b,),
        in_specs=[new, old, new, old, new, pl.BlockSpec((A_HEADS, t, w + t), lambda bi: (0, 0, 0))],
        out_specs=new,
        compiler_params=_params(("parallel",)),
        name="band_step",
    )(aq, pk, ak, pv, av, bias)


def _admissible(q0, q_rows, k_cols, k_true):
    kpos = lax.broadcasted_iota(jnp.int32, (q_rows, k_cols), 1)
    qpos = lax.broadcasted_iota(jnp.int32, (q_rows, k_cols), 0) + q0
    shift = CHUNK.bit_length() - 1
    return ((kpos >> shift) <= (qpos >> shift)) & (kpos < k_true)


def _select_topk(score, adm, topk):
    q_rows, l = score.shape
    bits = lax.bitcast_convert_type(score, jnp.int32)
    key = bits ^ ((bits >> 31) & jnp.int32(0x7FFFFFFF))
    kf = jnp.float32(topk)

    def body(it, prefix):
        cand = prefix | jnp.left_shift(jnp.int32(1), 31 - it)
        ge = key >= (cand ^ jnp.int32(INT_MIN))
        cnt = jnp.sum(jnp.where(ge, 1.0, 0.0), axis=1, keepdims=True)
        return jnp.where(cnt >= kf, cand, prefix)

    prefix = lax.fori_loop(0, 32, body, jnp.zeros((q_rows, 1), jnp.int32))
    thr = prefix ^ jnp.int32(INT_MIN)
    need = kf - jnp.sum(jnp.where(key > thr, 1.0, 0.0), axis=1, keepdims=True)
    rr = lax.broadcasted_iota(jnp.int32, (LANES, LANES), 0)
    cc = lax.broadcasted_iota(jnp.int32, (LANES, LANES), 1)
    before = jnp.where(rr < cc, 1.0, 0.0).astype(BF16)
    carry = jnp.zeros((q_rows, 1), F32)
    sel = []
    for t in range(l // LANES):
        key_t = key[:, t * LANES:(t + 1) * LANES]
        eq_t = jnp.where(key_t == thr, 1.0, 0.0)
        rank = carry + _dot(eq_t.astype(BF16), before)
        take_tie = jnp.where(rank < need, eq_t, 0.0)
        sel.append(jnp.where(key_t > thr, 1.0, take_tie))
        carry = carry + jnp.sum(eq_t, axis=1, keepdims=True)
    sel = jnp.concatenate(sel, axis=1)
    return jnp.where(adm, sel, 0.0)


def _dsa_attend(q0, k_true, topk, iq, iw, bq, ik_pieces, bk_pieces, bv_pieces, o_ref):
    qn = iq.shape[0]
    l = sum(_piece_len(p) for p in ik_pieces)
    score = None
    for h in range(IDX_HEADS):
        hs = slice(h * IDX_DIM, (h + 1) * IDX_DIM)
        dots = _cat([_qk(iq[:, hs], p) for p in ik_pieces])
        term = iw[:, h:h + 1] * jnp.maximum(dots, 0.0)
        score = term if score is None else score + term
    adm = _admissible(q0, qn, l, k_true)
    if k_true <= topk:
        mask = jnp.where(adm, 0.0, NEG)
    else:
        score = jnp.where(adm, score + 0.0, -jnp.inf)
        mask = jnp.where(_select_topk(score, adm, topk) > 0.0, 0.0, NEG)
    scale = HEAD_DIM ** -0.5
    for h in range(B_HEADS):
        hs = slice(h * HEAD_DIM, (h + 1) * HEAD_DIM)
        s = _cat([_qk(bq[:, hs], p) for p in bk_pieces]) * scale + mask
        m = s.max(axis=-1, keepdims=True)
        p = jnp.exp(s - m)
        den = p.sum(axis=-1, keepdims=True)
        o = _pv_pieces(p.astype(BF16), bv_pieces)
        o_ref[0, :, hs] = (o / den).astype(o_ref.dtype)


def _dsa_prompt_kernel(q0, k_len, topk, iq_ref, iw_ref, bq_ref, ikt_ref, bkt_ref, bvt_ref, o_ref):
    _dsa_attend(q0, k_len, topk, iq_ref[0], iw_ref[0], bq_ref[0],
                [("T", ikt_ref[0])], [("T", bkt_ref[0])], [("T", bvt_ref[0])], o_ref)


CAUSAL_QB = 256


def _dsa_prompt(iq, iw, bq, iktb, bktb, bvtb):
    b, s, _ = iq.shape
    qb = CAUSAL_QB
    topk = min(TOPK_MAX, s // 4)
    outs = []
    for k in range(s // qb):
        k_len = (k + 1) * qb
        qspec = lambda w, k=k: pl.BlockSpec((1, qb, w), lambda bi: (bi, k, 0))
        kspec = pl.BlockSpec((1, HEAD_DIM, k_len), lambda bi: (bi, 0, 0))
        outs.append(pl.pallas_call(
            functools.partial(_dsa_prompt_kernel, k * qb, k_len, topk),
            out_shape=jax.ShapeDtypeStruct((b, qb, B_WIDTH), BF16),
            grid=(b,),
            in_specs=[qspec(IDX_HEADS * IDX_DIM), qspec(IDX_HEADS), qspec(B_WIDTH), kspec, kspec, kspec],
            out_specs=pl.BlockSpec((1, qb, B_WIDTH), lambda bi: (bi, 0, 0)),
            compiler_params=_params(("parallel",)),
            name="dsa_prompt",
        )(iq, iw, bq, iktb, bktb, bvtb))
    return jnp.concatenate(outs, axis=1)


def _dsa_step_kernel(q0, k_true, topk, iq_ref, iw_ref, bq_ref, pik_ref, pbk_ref, pbv_ref,
                     ik_ref, bk_ref, bv_ref, o_ref):
    two = lambda old, new: [("T", old[0].astype(BF16)), ("N", new[0].astype(BF16))]
    _dsa_attend(q0, k_true, topk, iq_ref[0], iw_ref[0], bq_ref[0],
                two(pik_ref, ik_ref), two(pbk_ref, bk_ref), two(pbv_ref, bv_ref), o_ref)


def _dsa_step(iq, iw, bq, pik, pbk, pbv, ik, bk, bv, t_true):
    b, t, _ = iq.shape
    p_len = pik.shape[2]
    assert p_len % LANES == 0 and ik.shape[1] % LANES == 0
    k_true = p_len + t_true
    topk = min(TOPK_MAX, k_true // 4)
    qspec = lambda w: pl.BlockSpec((1, t, w), lambda bi: (bi, 0, 0))
    old = pl.BlockSpec((1, HEAD_DIM, p_len), lambda bi: (bi, 0, 0))
    new = pl.BlockSpec((1, ik.shape[1], HEAD_DIM), lambda bi: (bi, 0, 0))
    return pl.pallas_call(
        functools.partial(_dsa_step_kernel, p_len, k_true, topk),
        out_shape=jax.ShapeDtypeStruct((b, t, B_WIDTH), BF16),
        grid=(b,),
        in_specs=[qspec(IDX_HEADS * IDX_DIM), qspec(IDX_HEADS), qspec(B_WIDTH), old, old, old, new, new, new],
        out_specs=qspec(B_WIDTH),
        compiler_params=_params(("parallel",)),
        name="dsa_step",
    )(iq, iw, bq, pik, pbk, pbv, ik, bk, bv)


def _diff_attend(q0, k_true, lam_init, q, k_of, v_of, lam_ref, g_ref, o_ref):
    qn = q.shape[0]
    lp = lam_ref[...]
    lam = (jnp.exp(jnp.sum(lp[0:1] * lp[1:2], axis=1, keepdims=True))
           - jnp.exp(jnp.sum(lp[2:3] * lp[3:4], axis=1, keepdims=True)) + lam_init)
    l = sum(_piece_len(p) for p in k_of(0, 0))
    mask = jnp.where(_admissible(q0, qn, l, k_true), 0.0, NEG)
    scale = HEAD_DIM ** -0.5
    gain = g_ref[...] * (1.0 - lam_init)
    for h in range(C_HEADS):
        probs = []
        for part in range(2):
            cs = slice((2 * h + part) * HEAD_DIM, (2 * h + part + 1) * HEAD_DIM)
            s = _cat([_qk(q[:, cs], p) for p in k_of(h, part)]) * scale + mask
            m = s.max(axis=-1, keepdims=True)
            p = jnp.exp(s - m)
            probs.append(p * (1.0 / p.sum(axis=-1, keepdims=True)))
        a = probs[0] - lam * probs[1]
        o = _pv_pieces(a.astype(BF16), v_of(h))
        o = o * lax.rsqrt(jnp.mean(o * o, axis=-1, keepdims=True) + SUBLN_EPS) * gain
        o_ref[0, :, h * 2 * HEAD_DIM:(h + 1) * 2 * HEAD_DIM] = o.astype(o_ref.dtype)


def _diff_prompt_kernel(q0, k_len, lam_init, q_ref, kt_ref, v_ref, lam_ref, g_ref, o_ref):
    k_of = lambda h, part: [("T", kt_ref[0, (2 * h + part) * HEAD_DIM:(2 * h + part + 1) * HEAD_DIM, :])]
    v_of = lambda h: [("N", v_ref[0, :, h * 2 * HEAD_DIM:(h + 1) * 2 * HEAD_DIM])]
    _diff_attend(q0, k_len, lam_init, q_ref[0], k_of, v_of, lam_ref, g_ref, o_ref)


def _lam_init(layer):
    return 0.8 - 0.6 * math.exp(-0.3 * layer)


def _diff_prompt(cq, cktb, cvb, lam_p, subln_g, layer):
    b, s, _ = cq.shape
    qb = CAUSAL_QB
    small = [pl.BlockSpec((4, HEAD_DIM), lambda bi: (0, 0)), pl.BlockSpec((1, 2 * HEAD_DIM), lambda bi: (0, 0))]
    outs = []
    for k in range(s // qb):
        k_len = (k + 1) * qb
        outs.append(pl.pallas_call(
            functools.partial(_diff_prompt_kernel, k * qb, k_len, _lam_init(layer)),
            out_shape=jax.ShapeDtypeStruct((b, qb, C_WIDTH), BF16),
            grid=(b,),
            in_specs=[pl.BlockSpec((1, qb, C_WIDTH), functools.partial(lambda bi, k: (bi, k, 0), k=k)),
                      pl.BlockSpec((1, C_WIDTH, k_len), lambda bi: (bi, 0, 0)),
                      pl.BlockSpec((1, k_len, C_WIDTH), lambda bi: (bi, 0, 0))] + small,
            out_specs=pl.BlockSpec((1, qb, C_WIDTH), lambda bi: (bi, 0, 0)),
            compiler_params=_params(("parallel",)),
            name="diff_prompt",
        )(cq, cktb, cvb, lam_p, subln_g))
    return jnp.concatenate(outs, axis=1)


def _diff_step_kernel(q0, k_true, lam_init, q_ref, pk_ref, pv_ref, k_ref, v_ref, lam_ref, g_ref, o_ref):
    p_len = pk_ref.shape[1] // C_HEADS
    kn = k_ref[0].astype(BF16)
    vn = v_ref[0].astype(BF16)

    def k_of(h, part):
        old = pk_ref[0, pl.ds(h, p_len, stride=C_HEADS), :].astype(BF16)
        cs = slice((2 * h + part) * HEAD_DIM, (2 * h + part + 1) * HEAD_DIM)
        return [("N", old[:, part * HEAD_DIM:(part + 1) * HEAD_DIM]), ("N", kn[:, cs])]

    def v_of(h):
        old = pv_ref[0, pl.ds(h, p_len, stride=C_HEADS), :].astype(BF16)
        return [("N", old), ("N", vn[:, h * 2 * HEAD_DIM:(h + 1) * 2 * HEAD_DIM])]

    _diff_attend(q0, k_true, lam_init, q_ref[0], k_of, v_of, lam_ref, g_ref, o_ref)


def _diff_step(cq, pck, pcv, ck, cv, lam_p, subln_g, layer, t_true):
    b, t, _ = cq.shape
    p_len = pck.shape[1] // C_HEADS
    assert p_len % LANES == 0 and ck.shape[1] % LANES == 0
    qspec = pl.BlockSpec((1, t, C_WIDTH), lambda bi: (bi, 0, 0))
    old = pl.BlockSpec((1, p_len * C_HEADS, LANES), lambda bi: (bi, 0, 0))
    new = pl.BlockSpec((1, ck.shape[1], C_WIDTH), lambda bi: (bi, 0, 0))
    return pl.pallas_call(
        functools.partial(_diff_step_kernel, p_len, p_len + t_true, _lam_init(layer)),
        out_shape=jax.ShapeDtypeStruct((b, t, C_WIDTH), BF16),
        grid=(b,),
        in_specs=[qspec, old, old, new, new,
                  pl.BlockSpec((4, HEAD_DIM), lambda bi: (0, 0)), pl.BlockSpec((1, 2 * HEAD_DIM), lambda bi: (0, 0))],
        out_specs=qspec,
        compiler_params=_params(("parallel",)),
        name="diff_step",
    )(cq, pck, pcv, ck, cv, lam_p, subln_g)


def _merge_kernel(x_ref, ya_ref, yb_ref, yc_ref, wg_ref, wa_ref, wb_ref, wc_ref, wo_ref, g_ref, b_ref, o_ref):
    x = x_ref[...]
    xb = x.astype(BF16)
    merged = None
    for k, (y_ref, w_ref) in enumerate(((ya_ref, wa_ref), (yb_ref, wb_ref), (yc_ref, wc_ref))):
        gate = jax.nn.sigmoid(_dot(xb, wg_ref[:, k * D_MODEL:(k + 1) * D_MODEL]))
        term = gate * _dot(y_ref[...], w_ref[...])
        merged = term if merged is None else merged + term
    out = _dot(merged.astype(BF16), wo_ref[...])
    o_ref[...] = _layer_norm(ALPHA * x + out, g_ref[...], b_ref[...])


def _merge(x, ya, yb, yc, w_gates, wb_a, wb_b, wb_c, w_out, g, b):
    n = x.shape[0]
    tm = _row_tile(n, 512)
    row = lambda w: pl.BlockSpec((tm, w), lambda i: (i, 0))
    full = lambda a: pl.BlockSpec(a.shape, lambda i: (0, 0))
    return pl.pallas_call(
        _merge_kernel,
        out_shape=jax.ShapeDtypeStruct((n, D_MODEL), F32),
        grid=(n // tm,),
        in_specs=[row(D_MODEL), row(A_WIDTH), row(B_WIDTH), row(C_WIDTH),
                  full(w_gates), full(wb_a), full(wb_b), full(wb_c), full(w_out), full(g), full(b)],
        out_specs=row(D_MODEL),
        compiler_params=_params(("parallel",)),
        name="merge",
    )(x, ya, yb, yc, w_gates, wb_a, wb_b, wb_c, w_out, g, b)


def _mem_kernel(x_ref, mk_ref, mv_ref, wq_ref, wo_ref, g_ref, b_ref, o_ref, mkb_ref, mvb_ref):
    i = pl.program_id(1)

    @pl.when(i == 0)
    def _():
        mkb_ref[...] = mk_ref[0].astype(BF16)
        mvb_ref[...] = mv_ref[0].astype(BF16)

    x = x_ref[0]
    q = _dot(x.astype(BF16), wq_ref[...]).astype(BF16)
    scale = MEM_HEAD_DIM ** -0.5
    heads = []
    for h in range(MEM_HEADS):
        hs = slice(h * MEM_HEAD_DIM, (h + 1) * MEM_HEAD_DIM)
        s = _dot_nt(q[:, hs], mkb_ref[:, hs]) * scale
        m = s.max(axis=-1, keepdims=True)
        p = jnp.exp(s - m)
        den = p.sum(axis=-1, keepdims=True)
        heads.append((_dot(p.astype(BF16), mvb_ref[:, hs]) / den).astype(BF16))
    o = _dot(jnp.concatenate(heads, axis=1), wo_ref[...])
    o_ref[0] = _layer_norm(ALPHA * x + o, g_ref[...], b_ref[...])


def _mem_attn(x, mem_k, mem_v, w_q, w_o, g, b):
    bsz, t, _ = x.shape
    tq = _row_tile(t, 512)
    xspec = pl.BlockSpec((1, tq, D_MODEL), lambda bi, i: (bi, i, 0))
    mspec = pl.BlockSpec((1, N_MEM, D_MODEL), lambda bi, i: (bi, 0, 0))
    full = lambda a: pl.BlockSpec(a.shape, lambda bi, i: (0, 0))
    return pl.pallas_call(
        _mem_kernel,
        out_shape=jax.ShapeDtypeStruct((bsz, t, D_MODEL), F32),
        grid=(bsz, t // tq),
        in_specs=[xspec, mspec, mspec, full(w_q), full(w_o), full(g), full(b)],
        out_specs=xspec,
        scratch_shapes=[pltpu.VMEM((N_MEM, D_MODEL), BF16)] * 2,
        compiler_params=_params(("parallel", "arbitrary")),
        name="mem_attn",
    )(x, mem_k, mem_v, w_q, w_o, g, b)


def _memproj_kernel(x_ref, wk_ref, wv_ref, k_ref, v_ref):
    xb = x_ref[...].astype(BF16)
    k_ref[...] = _dot(xb, wk_ref[...])
    v_ref[...] = _dot(xb, wv_ref[...])


def _memproj(mem, w_k, w_v):
    n = mem.shape[0]
    tm = _row_tile(n, 512)
    row = pl.BlockSpec((tm, D_MODEL), lambda i: (i, 0))
    full = pl.BlockSpec((D_MODEL, D_MODEL), lambda i: (0, 0))
    return pl.pallas_call(
        _memproj_kernel,
        out_shape=(jax.ShapeDtypeStruct((n, D_MODEL), F32),) * 2,
        grid=(n // tm,),
        in_specs=[row, full, full],
        out_specs=(row, row),
        compiler_params=_params(("parallel",)),
        name="memproj",
    )(mem, w_k, w_v)


def _pad_rows(a, rows):
    pad = rows - a.shape[1]
    return a if pad == 0 else jnp.pad(a, ((0, 0), (0, pad), (0, 0)))


def _rest_of_layer(x2, ya, yb, yc, mem_k, mem_v, b, t, w):
    n = b * t
    x3 = _merge(x2, ya.reshape(n, -1), yb.reshape(n, -1), yc.reshape(n, -1),
                w["gates"], w["wb_a"], w["wb_b"], w["wb_c"], w["w_out"], w["ln_g"][1:2], w["ln_b"][1:2])
    x4 = _mem_attn(x3.reshape(b, t, D_MODEL), mem_k, mem_v, w["wm_q"], w["wm_o"], w["ln_g"][2:3], w["ln_b"][2:3])
    x5 = _ffn(x4.reshape(n, D_MODEL), w["f2_gu"], w["f2_d"], w["ln_g"][3:4], w["ln_b"][3:4])
    return x5.reshape(b, t, D_MODEL)


def _prompt_layer(x, mem_k, mem_v, layer, w):
    b, s, _ = x.shape
    n = b * s
    x2 = _ffn(x.reshape(n, D_MODEL), w["f1_gu"], w["f1_d"], w["ln_g"][0:1], w["ln_b"][0:1])
    tab = _rope_tables(jnp.arange(s, dtype=jnp.int32))
    (aq, akt, avt, aktb, avtb, bq, bkt, bvt, ikt, bktb, bvtb, iktb,
     iq, iw, cq, ck4, cv4, cktb, cvb) = _inproj_prompt(x2, w["in_pack"], tab, b, s)
    seq = lambda a: a.reshape(b, s, a.shape[-1])
    ya = _band_prompt(seq(aq), aktb, avtb, w["rel_bias"])
    yb = _dsa_prompt(seq(iq), seq(iw), seq(bq), iktb, bktb, bvtb)
    yc = _diff_prompt(seq(cq), cktb, seq(cvb), w["lam"], w["subln_g"], layer)
    y = _rest_of_layer(x2, ya, yb, yc, mem_k, mem_v, b, s, w)
    keep = min(A_WINDOW, s)
    heads_last = lambda a: jnp.transpose(a[:, :, s - keep:].reshape(b, A_HEADS, HEAD_DIM, keep), (0, 3, 1, 2))
    rows_last = lambda a: jnp.swapaxes(a, 1, 2)
    state = (heads_last(akt), heads_last(avt), rows_last(bkt), rows_last(bvt), rows_last(ikt),
             ck4.reshape(b, s, C_HEADS, 2 * HEAD_DIM), cv4.reshape(b, s, C_HEADS, 2 * HEAD_DIM))
    return y, state


def _step_layer(x, p_len, past, mem_k, mem_v, layer, w):
    b, t, _ = x.shape
    n = b * t
    x2 = _ffn(x.reshape(n, D_MODEL), w["f1_gu"], w["f1_d"], w["ln_g"][0:1], w["ln_b"][0:1])
    tm = _row_tile(n, INPROJ_TM)
    assert tm % t == 0
    tab = jnp.tile(_rope_tables(p_len + jnp.arange(t, dtype=jnp.int32)), (1, 1, tm // t, 1))
    outs = _inproj_step(x2, w["in_pack"], tab)
    aq, ak, av, bq, bk, bv, iq, ik, iw, cq, ck, cv = [o.reshape(b, t, o.shape[-1]) for o in outs]
    pa_k, pa_v, pb_k, pb_v, pb_i, pc_k, pc_v = past
    ya = _band_step(aq, ak, av, pa_k.reshape(b, -1, A_WIDTH), pa_v.reshape(b, -1, A_WIDTH), w["rel_bias"])
    t_pad = -(-t // LANES) * LANES
    pad = lambda a: _pad_rows(a, t_pad)
    keys_last = lambda a: jnp.swapaxes(a, 1, 2)
    yb = _dsa_step(iq, iw, bq, keys_last(pb_i), keys_last(pb_k), keys_last(pb_v), pad(ik), pad(bk), pad(bv), t)
    pairs = lambda a: a.reshape(b, a.shape[1] * C_HEADS, 2 * HEAD_DIM)
    yc = _diff_step(cq, pairs(pc_k), pairs(pc_v), pad(ck), pad(cv), w["lam"], w["subln_g"], layer, t)
    y = _rest_of_layer(x2, ya, yb, yc, mem_k, mem_v, b, t, w)
    state = (ak.reshape(b, t, A_HEADS, HEAD_DIM), av.reshape(b, t, A_HEADS, HEAD_DIM), bk, bv, ik,
             ck.reshape(b, t, C_HEADS, 2 * HEAD_DIM), cv.reshape(b, t, C_HEADS, 2 * HEAD_DIM))
    return y, state


def _layer_weights(l, ln_g, ln_b, ffn1_w_gu, ffn1_w_d, ffn2_w_gu, ffn2_w_d, w_in, a_rel_bias, c_lambda,
                   c_subln_g, w_branch_a, w_branch_b, w_branch_c, w_out, w_mem_q, w_mem_k, w_mem_v, w_mem_o):
    wi = w_in[l]
    iw_end = IN_OFFS[9]
    gates_at = IN_OFFS[12]
    in_pack = jnp.concatenate(
        [wi[:, :iw_end], jnp.zeros((D_MODEL, IW_PAD), wi.dtype), wi[:, iw_end:gates_at]], axis=1)
    bf = lambda a: a.astype(BF16)
    return dict(
        ln_g=ln_g[l].astype(F32), ln_b=ln_b[l].astype(F32),
        f1_gu=bf(ffn1_w_gu[l]), f1_d=bf(ffn1_w_d[l]), f2_gu=bf(ffn2_w_gu[l]), f2_d=bf(ffn2_w_d[l]),
        in_pack=bf(in_pack), gates=bf(wi[:, gates_at:]),
        rel_bias=a_rel_bias[l], lam=c_lambda[l].astype(F32), subln_g=c_subln_g[l].astype(F32)[None, :],
        wb_a=bf(w_branch_a[l]), wb_b=bf(w_branch_b[l]), wb_c=bf(w_branch_c[l]), w_out=bf(w_out[l]),
        wm_q=bf(w_mem_q[l]), wm_k=bf(w_mem_k[l]), wm_v=bf(w_mem_v[l]), wm_o=bf(w_mem_o[l]),
    )


def kernel(x_prompt, x_sample, cache_a_k, cache_a_v, cache_b_k, cache_b_v, cache_b_idx, cache_c_k, cache_c_v, cache_mem_k, cache_mem_v, mem_prompt, ln_g, ln_b, ffn1_w_gu, ffn1_w_d, ffn2_w_gu, ffn2_w_d, w_in, a_rel_bias, c_lambda, c_subln_g, w_branch_a, w_branch_b, w_branch_c, w_out, w_mem_q, w_mem_k, w_mem_v, w_mem_o):
    bp, s, _ = x_prompt.shape
    bs, t, _ = x_sample.shape
    p_len = cache_b_k.shape[2]
    depth = ln_g.shape[0]
    yp, ys = x_prompt, x_sample
    st_p, st_s, mk_p, mv_p = [], [], [], []
    for l in range(depth):
        w = _layer_weights(l, ln_g, ln_b, ffn1_w_gu, ffn1_w_d, ffn2_w_gu, ffn2_w_d, w_in, a_rel_bias,
                           c_lambda, c_subln_g, w_branch_a, w_branch_b, w_branch_c, w_out,
                           w_mem_q, w_mem_k, w_mem_v, w_mem_o)
        mk, mv = _memproj(mem_prompt.reshape(bp * N_MEM, D_MODEL), w["wm_k"], w["wm_v"])
        mk = mk.reshape(bp, N_MEM, D_MODEL)
        mv = mv.reshape(bp, N_MEM, D_MODEL)
        yp, sp = _prompt_layer(yp, mk, mv, l, w)
        past = (cache_a_k[l], cache_a_v[l], cache_b_k[l], cache_b_v[l], cache_b_idx[l],
                cache_c_k[l], cache_c_v[l])
        ys, ss = _step_layer(ys, p_len, past, cache_mem_k[l].reshape(bs, N_MEM, D_MODEL),
                             cache_mem_v[l].reshape(bs, N_MEM, D_MODEL), l, w)
        st_p.append(sp)
        st_s.append(ss)
        mk_p.append(mk.reshape(bp, N_MEM, MEM_HEADS, MEM_HEAD_DIM))
        mv_p.append(mv.reshape(bp, N_MEM, MEM_HEADS, MEM_HEAD_DIM))
    stk = lambda sts: [jnp.stack([st[i] for st in sts]) for i in range(7)]
    (a_k_p, a_v_p, b_k_p, b_v_p, b_i_p, c_k_p, c_v_p) = stk(st_p)
    (a_k_s, a_v_s, b_k_s, b_v_s, b_i_s, c_k_s, c_v_s) = stk(st_s)
    return (yp, ys, a_k_p, a_v_p, b_k_p, b_v_p, b_i_p, c_k_p, c_v_p,
            jnp.stack(mk_p), jnp.stack(mv_p), a_k_s, a_v_s, b_k_s, b_v_s, b_i_s, c_k_s, c_v_s)
```

```python
import functools
import math

import jax
import jax.numpy as jnp
import numpy as np
from jax import lax
from jax.experimental import pallas as pl
from jax.experimental.pallas import tpu as pltpu

F32 = jnp.float32
BF16 = jnp.bfloat16

D_MODEL = 1024
CHUNK = 64
HEAD_DIM = 64
ROT_DIM = HEAD_DIM // 4
ROT_HALF = ROT_DIM // 2
ROPE_THETA = 500000.0
A_HEADS = 4
A_LEFT_CHUNKS = 8
A_WINDOW = A_LEFT_CHUNKS * CHUNK
REL_CLIP = 128
B_HEADS = 4
IDX_HEADS = 8
IDX_DIM = 64
TOPK_MAX = 256
C_HEADS = 4
N_MEM = 256
MEM_HEADS = 4
MEM_HEAD_DIM = D_MODEL // MEM_HEADS
D_FF = ((8 * D_MODEL // 3 + 127) // 128) * 128
N_BRANCH = 3
DEPTH = 2
ALPHA = (2.0 * DEPTH) ** 0.25
LN_EPS = 1e-5
SUBLN_EPS = 1e-5
A_WIDTH = A_HEADS * HEAD_DIM
B_WIDTH = B_HEADS * HEAD_DIM
C_WIDTH = C_HEADS * 2 * HEAD_DIM
IN_SIZES = (A_WIDTH, A_WIDTH, A_WIDTH,
            B_WIDTH, HEAD_DIM, HEAD_DIM, IDX_HEADS * IDX_DIM, IDX_DIM, IDX_HEADS,
            C_WIDTH, C_WIDTH, C_WIDTH,
            N_BRANCH * D_MODEL)
IN_OFFS = tuple(int(v) for v in np.cumsum((0,) + IN_SIZES))

LANES = 128
V7X_VMEM_LIMIT = 56 * 1024 * 1024

NEG = -1e30
LOG2E = math.log2(math.e)
INT_MIN = -2 ** 31

IW_PAD = LANES - HEAD_DIM - IDX_HEADS
P_AQ, P_AK, P_AV = 0, 256, 512
P_B = 768
P_I = 1152
P_C = 1792
P_END = 3328


def _dot(a, b):
    return jnp.dot(a, b, preferred_element_type=F32)


def _dot_nt(a, b):
    return lax.dot_general(a, b, (((1,), (1,)), ((), ())), preferred_element_type=F32)


def _layer_norm(v, g, b):
    mu = jnp.mean(v, axis=-1, keepdims=True)
    d = v - mu
    var = jnp.mean(d * d, axis=-1, keepdims=True)
    return d * lax.rsqrt(var + LN_EPS) * g + b


def _params(sem):
    return pltpu.CompilerParams(dimension_semantics=sem, vmem_limit_bytes=V7X_VMEM_LIMIT)


def _row_tile(n, want):
    t = min(n, want)
    assert n % t == 0
    return t


def _ffn_kernel(x_ref, wg_ref, wu_ref, wd_ref, g_ref, b_ref, o_ref, xb_ref, acc_ref):
    j = pl.program_id(1)

    @pl.when(j == 0)
    def _():
        xb_ref[...] = x_ref[...].astype(BF16)
        acc_ref[...] = jnp.zeros_like(acc_ref)

    xb = xb_ref[...]
    g = _dot(xb, wg_ref[...])
    u = _dot(xb, wu_ref[...])
    h = g * jax.nn.sigmoid(g) * u
    acc_ref[...] += _dot(h.astype(BF16), wd_ref[...])

    @pl.when(j == pl.num_programs(1) - 1)
    def _():
        o_ref[...] = _layer_norm(ALPHA * x_ref[...] + 0.5 * acc_ref[...], g_ref[...], b_ref[...])


def _ffn(x, w_gu, w_d, g, b):
    n = x.shape[0]
    tm = _row_tile(n, 1024)
    tf = 256
    nf = D_FF // tf
    return pl.pallas_call(
        _ffn_kernel,
        out_shape=jax.ShapeDtypeStruct((n, D_MODEL), F32),
        grid=(n // tm, nf),
        in_specs=[
            pl.BlockSpec((tm, D_MODEL), lambda i, j: (i, 0)),
            pl.BlockSpec((D_MODEL, tf), lambda i, j: (0, j)),
            pl.BlockSpec((D_MODEL, tf), lambda i, j: (0, j + nf)),
            pl.BlockSpec((tf, D_MODEL), lambda i, j: (j, 0)),
            pl.BlockSpec((1, D_MODEL), lambda i, j: (0, 0)),
            pl.BlockSpec((1, D_MODEL), lambda i, j: (0, 0)),
        ],
        out_specs=pl.BlockSpec((tm, D_MODEL), lambda i, j: (i, 0)),
        scratch_shapes=[pltpu.VMEM((tm, D_MODEL), BF16), pltpu.VMEM((tm, D_MODEL), F32)],
        compiler_params=_params(("parallel", "arbitrary")),
        name="ffn",
    )(x, w_gu, w_gu, w_d, g, b)


def _rope_tables(pos):
    t = pos.shape[0]
    inv_freq = ROPE_THETA ** (-jnp.arange(ROT_HALF, dtype=F32) / ROT_HALF)
    ang = pos.astype(F32)[:, None] * inv_freq
    cos, sin = jnp.cos(ang), jnp.sin(ang)
    rest = HEAD_DIM - ROT_DIM
    ones = lambda w: jnp.ones((t, w), F32)
    zeros = lambda w: jnp.zeros((t, w), F32)
    c64 = jnp.concatenate([cos, cos, ones(rest)], axis=1)
    lo64 = jnp.concatenate([-sin, zeros(ROT_HALF), zeros(rest)], axis=1)
    hi64 = jnp.concatenate([zeros(ROT_HALF), sin, zeros(rest)], axis=1)
    iw64 = jnp.concatenate([jnp.full((t, IDX_HEADS), IDX_HEADS ** -0.5, F32), ones(HEAD_DIM - IDX_HEADS)], axis=1)
    cat = lambda a, b_: jnp.concatenate([a, b_], axis=1)
    both = jnp.stack([cat(c64, c64), cat(lo64, lo64), cat(hi64, hi64)])
    upper_id = jnp.stack([cat(c64, ones(HEAD_DIM)), cat(lo64, zeros(HEAD_DIM)), cat(hi64, zeros(HEAD_DIM))])
    upper_iw = jnp.stack([cat(c64, iw64), cat(lo64, zeros(HEAD_DIM)), cat(hi64, zeros(HEAD_DIM))])
    return jnp.stack([both, upper_id, upper_iw])


def _rope_apply(z, tab_ref, variant):
    cos = tab_ref[variant, 0]
    lo = tab_ref[variant, 1]
    hi = tab_ref[variant, 2]
    outs = []
    for k in range(z.shape[1] // LANES):
        zc = z[:, k * LANES:(k + 1) * LANES]
        outs.append(zc * cos + pltpu.roll(zc, LANES - ROT_HALF, 1) * lo + pltpu.roll(zc, ROT_HALF, 1) * hi)
    return outs[0] if len(outs) == 1 else jnp.concatenate(outs, axis=1)


def _inproj_sections(x_ref, w_ref, tab_ref):
    xb = x_ref[...].astype(BF16)
    za = _dot(xb, w_ref[:, P_AQ:P_B])
    zb = _dot(xb, w_ref[:, P_B:P_I])
    zi = _dot(xb, w_ref[:, P_I:P_C])
    zc = _dot(xb, w_ref[:, P_C:P_END])
    qs = HEAD_DIM ** -0.5 * LOG2E
    return dict(
        aq=za[:, 0:256] * qs, ak=za[:, 256:512], av=za[:, 512:768],
        bq=_rope_apply(zb[:, 0:256], tab_ref, 0) * qs,
        bkv=_rope_apply(zb[:, 256:384], tab_ref, 1),
        iq=_rope_apply(zi[:, 0:512], tab_ref, 0),
        ikw=_rope_apply(zi[:, 512:640], tab_ref, 2),
        cq=_rope_apply(zc[:, 0:512], tab_ref, 0) * qs,
        ck=_rope_apply(zc[:, 512:1024], tab_ref, 0),
        cv=zc[:, 1024:1536])


def _store_heads_interleaved(o_ref, v, heads):
    rows = v.shape[0]
    for h in range(heads):
        o_ref[pl.ds(h, rows, stride=heads), :] = v[:, h * LANES:(h + 1) * LANES]


def _inproj_prompt_kernel(x_ref, w_ref, tab_ref,
                          aq_ref, akt_ref, avt_ref, aktb_ref, avtb_ref,
                          bq_ref, bkt_ref, bvt_ref, ikt_ref, bktb_ref, bvtb_ref, iktb_ref,
                          iq_ref, iw_ref, cq_ref, ck4_ref, cv4_ref, cktb_ref, cvb_ref):
    s = _inproj_sections(x_ref, w_ref, tab_ref)
    aq_ref[...] = s["aq"].astype(BF16)
    akt = s["ak"].T
    avt = s["av"].T
    akt_ref[0] = akt
    avt_ref[0] = avt
    aktb_ref[0] = akt.astype(BF16)
    avtb_ref[0] = avt.astype(BF16)
    bq_ref[...] = s["bq"].astype(BF16)
    kvt = s["bkv"].T
    bkt_ref[0] = kvt[0:HEAD_DIM]
    bvt_ref[0] = kvt[HEAD_DIM:2 * HEAD_DIM]
    bktb_ref[0] = kvt[0:HEAD_DIM].astype(BF16)
    bvtb_ref[0] = kvt[HEAD_DIM:2 * HEAD_DIM].astype(BF16)
    ikt = s["ikw"].T[0:IDX_DIM]
    ikt_ref[0] = ikt
    iktb_ref[0] = ikt.astype(BF16)
    iq_ref[...] = s["iq"].astype(BF16)
    iw_ref[...] = s["ikw"][:, IDX_DIM:IDX_DIM + IDX_HEADS]
    cq_ref[...] = s["cq"].astype(BF16)
    _store_heads_interleaved(ck4_ref, s["ck"], C_HEADS)
    _store_heads_interleaved(cv4_ref, s["cv"], C_HEADS)
    cktb_ref[0] = s["ck"].T.astype(BF16)
    cvb_ref[...] = s["cv"].astype(BF16)


INPROJ_TM = 256


def _inproj_prompt(x, w_pack, tab, bsz, seq):
    n = x.shape[0]
    tm = INPROJ_TM
    spb = seq // tm
    tok = lambda w: pl.BlockSpec((tm, w), lambda i: (i, 0))
    tr = lambda d: pl.BlockSpec((1, d, tm), lambda i: (i // spb, 0, i % spb))
    sds = jax.ShapeDtypeStruct
    tok_o = lambda w, dt: (sds((n, w), dt), tok(w))
    tr_o = lambda d, dt: (sds((bsz, d, seq), dt), tr(d))
    il_o = (sds((n * C_HEADS, LANES), F32), pl.BlockSpec((tm * C_HEADS, LANES), lambda i: (i, 0)))
    outs = [tok_o(A_WIDTH, BF16), tr_o(A_WIDTH, F32), tr_o(A_WIDTH, F32), tr_o(A_WIDTH, BF16), tr_o(A_WIDTH, BF16),
            tok_o(B_WIDTH, BF16), tr_o(HEAD_DIM, F32), tr_o(HEAD_DIM, F32), tr_o(IDX_DIM, F32),
            tr_o(HEAD_DIM, BF16), tr_o(HEAD_DIM, BF16), tr_o(IDX_DIM, BF16),
            tok_o(IDX_HEADS * IDX_DIM, BF16), tok_o(IDX_HEADS, F32), tok_o(C_WIDTH, BF16),
            il_o, il_o, tr_o(C_WIDTH, BF16), tok_o(C_WIDTH, BF16)]
    return pl.pallas_call(
        _inproj_prompt_kernel,
        out_shape=tuple(o[0] for o in outs),
        grid=(n // tm,),
        in_specs=[tok(D_MODEL),
                  pl.BlockSpec((D_MODEL, P_END), lambda i: (0, 0)),
                  pl.BlockSpec((3, 3, tm, LANES), lambda i: (0, 0, i % spb, 0))],
        out_specs=tuple(o[1] for o in outs),
        compiler_params=_params(("parallel",)),
        name="inproj_prompt",
    )(x, w_pack, tab)


def _inproj_step_kernel(x_ref, w_ref, tab_ref,
                        aq_ref, ak_ref, av_ref, bq_ref, bk_ref, bv_ref,
                        iq_ref, ik_ref, iw_ref, cq_ref, ck_ref, cv_ref):
    s = _inproj_sections(x_ref, w_ref, tab_ref)
    aq_ref[...] = s["aq"].astype(BF16)
    ak_ref[...] = s["ak"]
    av_ref[...] = s["av"]
    bq_ref[...] = s["bq"].astype(BF16)
    bk_ref[...] = s["bkv"][:, 0:HEAD_DIM]
    bv_ref[...] = s["bkv"][:, HEAD_DIM:2 * HEAD_DIM]
    iq_ref[...] = s["iq"].astype(BF16)
    ik_ref[...] = s["ikw"][:, 0:IDX_DIM]
    iw_ref[...] = s["ikw"][:, IDX_DIM:IDX_DIM + IDX_HEADS]
    cq_ref[...] = s["cq"].astype(BF16)
    ck_ref[...] = s["ck"]
    cv_ref[...] = s["cv"]


def _inproj_step(x, w_pack, tab):
    n = x.shape[0]
    tm = _row_tile(n, INPROJ_TM)
    widths = ((A_WIDTH, BF16), (A_WIDTH, F32), (A_WIDTH, F32), (B_WIDTH, BF16), (HEAD_DIM, F32), (HEAD_DIM, F32),
              (IDX_HEADS * IDX_DIM, BF16), (IDX_DIM, F32), (IDX_HEADS, F32),
              (C_WIDTH, BF16), (C_WIDTH, F32), (C_WIDTH, F32))
    return pl.pallas_call(
        _inproj_step_kernel,
        out_shape=tuple(jax.ShapeDtypeStruct((n, w), dt) for w, dt in widths),
        grid=(n // tm,),
        in_specs=[
            pl.BlockSpec((tm, D_MODEL), lambda i: (i, 0)),
            pl.BlockSpec((D_MODEL, P_END), lambda i: (0, 0)),
            pl.BlockSpec((3, 3, tm, LANES), lambda i: (0, 0, 0, 0)),
        ],
        out_specs=tuple(pl.BlockSpec((tm, w), lambda i: (i, 0)) for w, _ in widths),
        compiler_params=_params(("parallel",)),
        name="inproj_step",
    )(x, w_pack, tab)


def _qk(q, piece):
    kind, k = piece
    return _dot(q, k) if kind == "T" else _dot_nt(q, k)


def _pv(p, piece):
    kind, v = piece
    return _dot_nt(p, v) if kind == "T" else _dot(p, v)


def _piece_len(piece):
    kind, a = piece
    return a.shape[1] if kind == "T" else a.shape[0]


def _cat(parts):
    return parts[0] if len(parts) == 1 else jnp.concatenate(parts, axis=1)


def _pv_pieces(p, pieces):
    off = 0
    o = None
    for piece in pieces:
        n = _piece_len(piece)
        t = _pv(p[:, off:off + n], piece)
        o = t if o is None else o + t
        off += n
    return o


def _toeplitz_bias(rel_bias, q_rows, k_cols, q_off):
    period = pl.next_power_of_2(q_rows + k_cols)
    e = np.arange(period)
    e = np.where(e >= k_cols, e - period, e)
    idx = np.clip(q_off - e, -REL_CLIP, REL_CLIP) + REL_CLIP
    table = rel_bias[:, idx].astype(F32) * LOG2E
    flat = jnp.tile(table, (1, q_rows + 1))[:, :q_rows * (period - 1)]
    return flat.reshape(rel_bias.shape[0], q_rows, period - 1)[:, :, :k_cols]


def _band_attend(q, pieces, bias_ref, o_ref):
    for h in range(A_HEADS):
        hs = slice(h * HEAD_DIM, (h + 1) * HEAD_DIM)
        qh = q[:, hs]
        ss = []
        off = 0
        for k_of, _, extra in pieces:
            kp = k_of(h)
            n = _piece_len(kp)
            s = _qk(qh, kp) + bias_ref[h, :, off:off + n]
            if extra is not None:
                s = s + extra
            ss.append(s)
            off += n
        m = ss[0].max(axis=-1, keepdims=True)
        for s in ss[1:]:
            m = jnp.maximum(m, s.max(axis=-1, keepdims=True))
        l = jnp.zeros_like(m)
        o = jnp.zeros((q.shape[0], HEAD_DIM), F32)
        for s, (_, v_of, _) in zip(ss, pieces):
            p = jnp.exp2(s - m)
            l = l + p.sum(axis=-1, keepdims=True)
            o = o + _pv(p.astype(BF16), v_of(h))
        o_ref[0, :, hs] = (o / l).astype(o_ref.dtype)


BAND_QB = 256
BAND_KB = 3


def _band_prompt_kernel(q_ref, k0_ref, k1_ref, k2_ref, v0_ref, v1_ref, v2_ref, bias_ref, o_ref):
    j = pl.program_id(1)
    pieces = []
    for kb, (k_ref, v_ref) in enumerate(((k0_ref, v0_ref), (k1_ref, v1_ref), (k2_ref, v2_ref))):
        back = BAND_KB - 1 - kb
        extra = None if back == 0 else jnp.where(j >= back, 0.0, NEG)
        head_rows = lambda ref: (lambda h: ("T", ref[0, h * HEAD_DIM:(h + 1) * HEAD_DIM, :]))
        pieces.append((head_rows(k_ref), head_rows(v_ref), extra))
    _band_attend(q_ref[0], pieces, bias_ref, o_ref)


def _band_prompt(aq, akt, avt, rel_bias):
    b, s, _ = aq.shape
    assert A_WINDOW == (BAND_KB - 1) * BAND_QB and s % BAND_QB == 0
    kw = BAND_KB * BAND_QB
    r = np.arange(BAND_QB)[:, None] + A_WINDOW
    c = np.arange(kw)[None, :]
    in_band = (c // CHUNK <= r // CHUNK) & (c // CHUNK >= r // CHUNK - A_LEFT_CHUNKS)
    bias = jnp.where(in_band[None], _toeplitz_bias(rel_bias, BAND_QB, kw, A_WINDOW), NEG)
    qspec = pl.BlockSpec((1, BAND_QB, A_WIDTH), lambda bi, j: (bi, j, 0))
    kspecs = [pl.BlockSpec((1, A_WIDTH, BAND_QB),
                           functools.partial(lambda bi, j, back: (bi, 0, jnp.maximum(j - back, 0)),
                                             back=BAND_KB - 1 - kb))
              for kb in range(BAND_KB)]
    return pl.pallas_call(
        _band_prompt_kernel,
        out_shape=jax.ShapeDtypeStruct((b, s, A_WIDTH), BF16),
        grid=(b, s // BAND_QB),
        in_specs=[qspec] + kspecs + kspecs + [pl.BlockSpec((A_HEADS, BAND_QB, kw), lambda bi, j: (0, 0, 0))],
        out_specs=qspec,
        compiler_params=_params(("parallel", "parallel")),
        name="band_prompt",
    )(aq, akt, akt, akt, avt, avt, avt, bias)


def _band_step_kernel(q_ref, pk_ref, k_ref, pv_ref, v_ref, bias_ref, o_ref):
    head_cols = lambda a: (lambda h: ("N", a[:, h * HEAD_DIM:(h + 1) * HEAD_DIM]))
    pieces = [(head_cols(pk_ref[0].astype(BF16)), head_cols(pv_ref[0].astype(BF16)), None),
              (head_cols(k_ref[0].astype(BF16)), head_cols(v_ref[0].astype(BF16)), None)]
    _band_attend(q_ref[0], pieces, bias_ref, o_ref)


def _band_step(aq, ak, av, pk, pv, rel_bias, first):
    b, t, _ = aq.shape
    w = pk.shape[1]
    bias = _toeplitz_bias(rel_bias, t, w + t, w)
    new = pl.BlockSpec((1, t, A_WIDTH), lambda bi: (bi, 0, 0))
    old = pl.BlockSpec((1, w, A_WIDTH), lambda bi: (first + bi, 0, 0))
    return pl.pallas_call(
        _band_step_kernel,
        out_shape=jax.ShapeDtypeStruct((b, t, A_WIDTH), BF16),
        grid=(b,),
        in_specs=[new, old, new, old, new, pl.BlockSpec((A_HEADS, t, w + t), lambda bi: (0, 0, 0))],
        out_specs=new,
        compiler_params=_params(("parallel",)),
        name="band_step",
    )(aq, pk, ak, pv, av, bias)


def _admissible(q0, q_rows, k0, k_cols, k_true):
    kpos = lax.broadcasted_iota(jnp.int32, (q_rows, k_cols), 1) + k0
    qpos = lax.broadcasted_iota(jnp.int32, (q_rows, k_cols), 0) + q0
    shift = CHUNK.bit_length() - 1
    return ((kpos >> shift) <= (qpos >> shift)) & (kpos < k_true)


def _free_cols(q0, k_cols, k_true):
    return min((q0 // CHUNK + 1) * CHUNK, k_true, k_cols) // LANES * LANES


def _count_ge(ref, cand16, width):
    acc = None
    for t in range(width // LANES):
        one = jnp.where(ref[:, t * LANES:(t + 1) * LANES] >= cand16, jnp.int16(1), jnp.int16(0))
        acc = one if acc is None else acc + one
    return jnp.sum(acc.astype(F32), axis=1, keepdims=True)


def _search16(ref, want, width):
    def body(it, prefix):
        cand = prefix | jnp.left_shift(jnp.int32(1), 15 - it)
        cnt = _count_ge(ref, (cand - 32768).astype(jnp.int16), width)
        return jnp.where(cnt >= want, cand, prefix)
    return lax.fori_loop(0, 16, body, jnp.zeros((ref.shape[0], 1), jnp.int32))


def _select_topk(score, topk, hi_ref, lo_ref):
    q_rows, l = score.shape
    bits = lax.bitcast_convert_type(score, jnp.int32)
    key = bits ^ ((bits >> 31) & jnp.int32(0x7FFFFFFF))
    kf = jnp.float32(topk)
    hi_ref[...] = (key >> 16).astype(jnp.int16)
    hi_thr = _search16(hi_ref, kf, l) - 32768
    hi_thr16 = hi_thr.astype(jnp.int16)
    above = _count_ge(hi_ref, hi_thr16 + jnp.int16(1), l) * jnp.where(hi_thr < 32767, 1.0, 0.0)
    low = ((key & jnp.int32(0xFFFF)) - 32768).astype(jnp.int16)
    lo_ref[...] = jnp.where(hi_ref[...] == hi_thr16, low, jnp.int16(-32768))
    lo_thr = _search16(lo_ref, kf - above, l)
    thr = (hi_thr << 16) | lo_thr
    need = kf - jnp.sum(jnp.where(key > thr, 1.0, 0.0), axis=1, keepdims=True)
    rr = lax.broadcasted_iota(jnp.int32, (LANES, LANES), 0)
    cc = lax.broadcasted_iota(jnp.int32, (LANES, LANES), 1)
    before = jnp.where(rr < cc, 1.0, 0.0).astype(BF16)
    carry = jnp.zeros((q_rows, 1), F32)
    sel = []
    for t in range(l // LANES):
        key_t = key[:, t * LANES:(t + 1) * LANES]
        eq_t = jnp.where(key_t == thr, 1.0, 0.0)
        rank = carry + _dot(eq_t.astype(BF16), before)
        take_tie = jnp.where(rank < need, eq_t, 0.0)
        sel.append(jnp.where(key_t > thr, 1.0, take_tie))
        carry = carry + jnp.sum(eq_t, axis=1, keepdims=True)
    return jnp.concatenate(sel, axis=1)


def _dsa_attend(q0, k_true, topk, iq, iw, bq, ik_pieces, bk_pieces, bv_pieces, o_ref, hi_ref, lo_ref):
    qn = iq.shape[0]
    l = sum(_piece_len(p) for p in ik_pieces)
    score = None
    for h in range(IDX_HEADS):
        hs = slice(h * IDX_DIM, (h + 1) * IDX_DIM)
        dots = _cat([_qk(iq[:, hs], p) for p in ik_pieces])
        term = iw[:, h:h + 1] * jnp.maximum(dots, 0.0)
        score = term if score is None else score + term
    free = _free_cols(q0, l, k_true)
    adm = _admissible(q0, qn, free, l - free, k_true)
    if k_true <= topk:
        tail = jnp.where(adm, 0.0, NEG)
        mask = tail if free == 0 else jnp.concatenate([jnp.zeros((qn, free), F32), tail], axis=1)
    else:
        score = score + 0.0
        tail = jnp.where(adm, score[:, free:], -jnp.inf)
        score = tail if free == 0 else jnp.concatenate([score[:, :free], tail], axis=1)
        sel = _select_topk(score, topk, hi_ref, lo_ref)
        tail = jnp.where(adm, jnp.where(sel[:, free:] > 0.0, 0.0, NEG), NEG)
        mask = tail if free == 0 else jnp.concatenate([jnp.where(sel[:, :free] > 0.0, 0.0, NEG), tail], axis=1)
    for h in range(B_HEADS):
        hs = slice(h * HEAD_DIM, (h + 1) * HEAD_DIM)
        s = _cat([_qk(bq[:, hs], p) for p in bk_pieces]) + mask
        m = s.max(axis=-1, keepdims=True)
        p = jnp.exp2(s - m)
        den = p.sum(axis=-1, keepdims=True)
        o = _pv_pieces(p.astype(BF16), bv_pieces)
        o_ref[0, :, hs] = (o / den).astype(o_ref.dtype)


def _dsa_prompt_kernel(q0, k_len, topk, iq_ref, iw_ref, bq_ref, ikt_ref, bkt_ref, bvt_ref, o_ref, hi_ref, lo_ref):
    _dsa_attend(q0, k_len, topk, iq_ref[0], iw_ref[0], bq_ref[0],
                [("T", ikt_ref[0])], [("T", bkt_ref[0])], [("T", bvt_ref[0])], o_ref, hi_ref, lo_ref)


CAUSAL_QB = 256


def _dsa_prompt(iq, iw, bq, iktb, bktb, bvtb):
    b, s, _ = iq.shape
    qb = CAUSAL_QB
    topk = min(TOPK_MAX, s // 4)
    outs = []
    for k in range(s // qb):
        k_len = (k + 1) * qb
        qspec = lambda w, k=k: pl.BlockSpec((1, qb, w), lambda bi: (bi, k, 0))
        kspec = pl.BlockSpec((1, HEAD_DIM, k_len), lambda bi: (bi, 0, 0))
        outs.append(pl.pallas_call(
            functools.partial(_dsa_prompt_kernel, k * qb, k_len, topk),
            out_shape=jax.ShapeDtypeStruct((b, qb, B_WIDTH), BF16),
            grid=(b,),
            in_specs=[qspec(IDX_HEADS * IDX_DIM), qspec(IDX_HEADS), qspec(B_WIDTH), kspec, kspec, kspec],
            out_specs=pl.BlockSpec((1, qb, B_WIDTH), lambda bi: (bi, 0, 0)),
            scratch_shapes=[pltpu.VMEM((qb, k_len), jnp.int16)] * 2,
            compiler_params=_params(("parallel",)),
            name="dsa_prompt",
        )(iq, iw, bq, iktb, bktb, bvtb))
    return jnp.concatenate(outs, axis=1)


def _dsa_step_kernel(q0, k_true, topk, iq_ref, iw_ref, bq_ref, pik_ref, pbk_ref, pbv_ref,
                     ik_ref, bk_ref, bv_ref, o_ref, hi_ref, lo_ref):
    two = lambda old, new: [("T", old[0].astype(BF16)), ("N", new[0].astype(BF16))]
    _dsa_attend(q0, k_true, topk, iq_ref[0], iw_ref[0], bq_ref[0],
                two(pik_ref, ik_ref), two(pbk_ref, bk_ref), two(pbv_ref, bv_ref), o_ref, hi_ref, lo_ref)


def _dsa_step(iq, iw, bq, pik, pbk, pbv, ik, bk, bv, t_true, first):
    b, t, _ = iq.shape
    p_len = pik.shape[2]
    assert p_len % LANES == 0 and ik.shape[1] % LANES == 0
    k_true = p_len + t_true
    topk = min(TOPK_MAX, k_true // 4)
    qspec = lambda w: pl.BlockSpec((1, t, w), lambda bi: (bi, 0, 0))
    old = pl.BlockSpec((1, HEAD_DIM, p_len), lambda bi: (first + bi, 0, 0))
    new = pl.BlockSpec((1, ik.shape[1], HEAD_DIM), lambda bi: (bi, 0, 0))
    return pl.pallas_call(
        functools.partial(_dsa_step_kernel, p_len, k_true, topk),
        out_shape=jax.ShapeDtypeStruct((b, t, B_WIDTH), BF16),
        grid=(b,),
        in_specs=[qspec(IDX_HEADS * IDX_DIM), qspec(IDX_HEADS), qspec(B_WIDTH), old, old, old, new, new, new],
        out_specs=qspec(B_WIDTH),
        scratch_shapes=[pltpu.VMEM((t, p_len + ik.shape[1]), jnp.int16)] * 2,
        compiler_params=_params(("parallel",)),
        name="dsa_step",
    )(iq, iw, bq, pik, pbk, pbv, ik, bk, bv)


def _diff_attend(q0, k_true, lam_init, q, k_of, v_of, lam_ref, g_ref, o_ref):
    qn = q.shape[0]
    lp = lam_ref[...]
    lam = (jnp.exp(jnp.sum(lp[0:1] * lp[1:2], axis=1, keepdims=True))
           - jnp.exp(jnp.sum(lp[2:3] * lp[3:4], axis=1, keepdims=True)) + lam_init)
    l = sum(_piece_len(p) for p in k_of(0, 0))
    free = _free_cols(q0, l, k_true)
    tail = jnp.where(_admissible(q0, qn, free, l - free, k_true), 0.0, NEG)
    gain = g_ref[...] * (1.0 - lam_init)
    for h in range(C_HEADS):
        es, invs = [], []
        for part in range(2):
            cs = slice((2 * h + part) * HEAD_DIM, (2 * h + part + 1) * HEAD_DIM)
            s = _cat([_qk(q[:, cs], p) for p in k_of(h, part)])
            s = s + tail if free == 0 else jnp.concatenate([s[:, :free], s[:, free:] + tail], axis=1)
            m = s.max(axis=-1, keepdims=True)
            e = jnp.exp2(s - m)
            es.append(e)
            invs.append(1.0 / e.sum(axis=-1, keepdims=True))
        a = es[0] * invs[0] - es[1] * (lam * invs[1])
        o = _pv_pieces(a.astype(BF16), v_of(h))
        o = o * lax.rsqrt(jnp.mean(o * o, axis=-1, keepdims=True) + SUBLN_EPS) * gain
        o_ref[0, :, h * 2 * HEAD_DIM:(h + 1) * 2 * HEAD_DIM] = o.astype(o_ref.dtype)


def _diff_prompt_kernel(q0, k_len, lam_init, q_ref, kt_ref, v_ref, lam_ref, g_ref, o_ref):
    k_of = lambda h, part: [("T", kt_ref[0, (2 * h + part) * HEAD_DIM:(2 * h + part + 1) * HEAD_DIM, :])]
    v_of = lambda h: [("N", v_ref[0, :, h * 2 * HEAD_DIM:(h + 1) * 2 * HEAD_DIM])]
    _diff_attend(q0, k_len, lam_init, q_ref[0], k_of, v_of, lam_ref, g_ref, o_ref)


def _lam_init(layer):
    return 0.8 - 0.6 * math.exp(-0.3 * layer)


def _diff_prompt(cq, cktb, cvb, lam_p, subln_g, layer):
    b, s, _ = cq.shape
    qb = CAUSAL_QB
    small = [pl.BlockSpec((4, HEAD_DIM), lambda bi: (0, 0)), pl.BlockSpec((1, 2 * HEAD_DIM), lambda bi: (0, 0))]
    outs = []
    for k in range(s // qb):
        k_len = (k + 1) * qb
        outs.append(pl.pallas_call(
            functools.partial(_diff_prompt_kernel, k * qb, k_len, _lam_init(layer)),
            out_shape=jax.ShapeDtypeStruct((b, qb, C_WIDTH), BF16),
            grid=(b,),
            in_specs=[pl.BlockSpec((1, qb, C_WIDTH), functools.partial(lambda bi, k: (bi, k, 0), k=k)),
                      pl.BlockSpec((1, C_WIDTH, k_len), lambda bi: (bi, 0, 0)),
                      pl.BlockSpec((1, k_len, C_WIDTH), lambda bi: (bi, 0, 0))] + small,
            out_specs=pl.BlockSpec((1, qb, C_WIDTH), lambda bi: (bi, 0, 0)),
            compiler_params=_params(("parallel",)),
            name="diff_prompt",
        )(cq, cktb, cvb, lam_p, subln_g))
    return jnp.concatenate(outs, axis=1)


def _diff_step_kernel(q0, k_true, lam_init, q_ref, pk_ref, pv_ref, k_ref, v_ref, lam_ref, g_ref, o_ref):
    p_len = pk_ref.shape[1] // C_HEADS
    kn = k_ref[0].astype(BF16)
    vn = v_ref[0].astype(BF16)

    def k_of(h, part):
        old = pk_ref[0, pl.ds(h, p_len, stride=C_HEADS), :].astype(BF16)
        cs = slice((2 * h + part) * HEAD_DIM, (2 * h + part + 1) * HEAD_DIM)
        return [("N", old[:, part * HEAD_DIM:(part + 1) * HEAD_DIM]), ("N", kn[:, cs])]

    def v_of(h):
        old = pv_ref[0, pl.ds(h, p_len, stride=C_HEADS), :].astype(BF16)
        return [("N", old), ("N", vn[:, h * 2 * HEAD_DIM:(h + 1) * 2 * HEAD_DIM])]

    _diff_attend(q0, k_true, lam_init, q_ref[0], k_of, v_of, lam_ref, g_ref, o_ref)


def _diff_step(cq, pck, pcv, ck, cv, lam_p, subln_g, layer, t_true, first):
    b, t, _ = cq.shape
    p_len = pck.shape[1] // C_HEADS
    assert p_len % LANES == 0 and ck.shape[1] % LANES == 0
    qspec = pl.BlockSpec((1, t, C_WIDTH), lambda bi: (bi, 0, 0))
    old = pl.BlockSpec((1, p_len * C_HEADS, LANES), lambda bi: (first + bi, 0, 0))
    new = pl.BlockSpec((1, ck.shape[1], C_WIDTH), lambda bi: (bi, 0, 0))
    return pl.pallas_call(
        functools.partial(_diff_step_kernel, p_len, p_len + t_true, _lam_init(layer)),
        out_shape=jax.ShapeDtypeStruct((b, t, C_WIDTH), BF16),
        grid=(b,),
        in_specs=[qspec, old, old, new, new,
                  pl.BlockSpec((4, HEAD_DIM), lambda bi: (0, 0)), pl.BlockSpec((1, 2 * HEAD_DIM), lambda bi: (0, 0))],
        out_specs=qspec,
        compiler_params=_params(("parallel",)),
        name="diff_step",
    )(cq, pck, pcv, ck, cv, lam_p, subln_g)


def _merge_kernel(x_ref, ya_ref, yb_ref, yc_ref, wg_ref, wa_ref, wb_ref, wc_ref, wo_ref, g_ref, b_ref, o_ref):
    x = x_ref[...]
    xb = x.astype(BF16)
    merged = None
    for k, (y_ref, w_ref) in enumerate(((ya_ref, wa_ref), (yb_ref, wb_ref), (yc_ref, wc_ref))):
        gate = jax.nn.sigmoid(_dot(xb, wg_ref[:, k * D_MODEL:(k + 1) * D_MODEL]))
        term = gate * _dot(y_ref[...], w_ref[...])
        merged = term if merged is None else merged + term
    out = _dot(merged.astype(BF16), wo_ref[...])
    o_ref[...] = _layer_norm(ALPHA * x + out, g_ref[...], b_ref[...])


def _merge(x, ya, yb, yc, w_gates, wb_a, wb_b, wb_c, w_out, g, b):
    n = x.shape[0]
    tm = _row_tile(n, 512)
    row = lambda w: pl.BlockSpec((tm, w), lambda i: (i, 0))
    full = lambda a: pl.BlockSpec(a.shape, lambda i: (0, 0))
    return pl.pallas_call(
        _merge_kernel,
        out_shape=jax.ShapeDtypeStruct((n, D_MODEL), F32),
        grid=(n // tm,),
        in_specs=[row(D_MODEL), row(A_WIDTH), row(B_WIDTH), row(C_WIDTH),
                  full(w_gates), full(wb_a), full(wb_b), full(wb_c), full(w_out), full(g), full(b)],
        out_specs=row(D_MODEL),
        compiler_params=_params(("parallel",)),
        name="merge",
    )(x, ya, yb, yc, w_gates, wb_a, wb_b, wb_c, w_out, g, b)


def _mem_kernel(x_ref, mk_ref, mv_ref, wq_ref, wo_ref, g_ref, b_ref, o_ref, mkb_ref, mvb_ref):
    i = pl.program_id(1)

    @pl.when(i == 0)
    def _():
        mkb_ref[...] = mk_ref[0].astype(BF16)
        mvb_ref[...] = mv_ref[0].astype(BF16)

    x = x_ref[0]
    q = _dot(x.astype(BF16), wq_ref[...]).astype(BF16)
    scale = MEM_HEAD_DIM ** -0.5
    heads = []
    for h in range(MEM_HEADS):
        hs = slice(h * MEM_HEAD_DIM, (h + 1) * MEM_HEAD_DIM)
        s = _dot_nt(q[:, hs], mkb_ref[:, hs]) * scale
        m = s.max(axis=-1, keepdims=True)
        p = jnp.exp(s - m)
        den = p.sum(axis=-1, keepdims=True)
        heads.append((_dot(p.astype(BF16), mvb_ref[:, hs]) / den).astype(BF16))
    o = _dot(jnp.concatenate(heads, axis=1), wo_ref[...])
    o_ref[0] = _layer_norm(ALPHA * x + o, g_ref[...], b_ref[...])


def _mem_attn(x, mem_k, mem_v, first, w_q, w_o, g, b):
    bsz, t, _ = x.shape
    tq = _row_tile(t, 512)
    xspec = pl.BlockSpec((1, tq, D_MODEL), lambda bi, i: (bi, i, 0))
    mspec = pl.BlockSpec((1, N_MEM, D_MODEL), lambda bi, i: (first + bi, 0, 0))
    full = lambda a: pl.BlockSpec(a.shape, lambda bi, i: (0, 0))
    return pl.pallas_call(
        _mem_kernel,
        out_shape=jax.ShapeDtypeStruct((bsz, t, D_MODEL), F32),
        grid=(bsz, t // tq),
        in_specs=[xspec, mspec, mspec, full(w_q), full(w_o), full(g), full(b)],
        out_specs=xspec,
        scratch_shapes=[pltpu.VMEM((N_MEM, D_MODEL), BF16)] * 2,
        compiler_params=_params(("parallel", "arbitrary")),
        name="mem_attn",
    )(x, mem_k, mem_v, w_q, w_o, g, b)


def _memproj_kernel(x_ref, wk_ref, wv_ref, k_ref, v_ref):
    xb = x_ref[...].astype(BF16)
    k_ref[...] = _dot(xb, wk_ref[...])
    v_ref[...] = _dot(xb, wv_ref[...])


def _memproj(mem, w_k, w_v):
    n = mem.shape[0]
    tm = _row_tile(n, 512)
    row = pl.BlockSpec((tm, D_MODEL), lambda i: (i, 0))
    full = pl.BlockSpec((D_MODEL, D_MODEL), lambda i: (0, 0))
    return pl.pallas_call(
        _memproj_kernel,
        out_shape=(jax.ShapeDtypeStruct((n, D_MODEL), F32),) * 2,
        grid=(n // tm,),
        in_specs=[row, full, full],
        out_specs=(row, row),
        compiler_params=_params(("parallel",)),
        name="memproj",
    )(mem, w_k, w_v)


def _pad_rows(a, rows):
    pad = rows - a.shape[1]
    return a if pad == 0 else jnp.pad(a, ((0, 0), (0, pad), (0, 0)))


def _rest_of_layer(x2, ya, yb, yc, mem_k, mem_v, mem_first, b, t, w):
    n = b * t
    x3 = _merge(x2, ya.reshape(n, -1), yb.reshape(n, -1), yc.reshape(n, -1),
                w["gates"], w["wb_a"], w["wb_b"], w["wb_c"], w["w_out"], w["ln_g"][1:2], w["ln_b"][1:2])
    x4 = _mem_attn(x3.reshape(b, t, D_MODEL), mem_k, mem_v, mem_first, w["wm_q"], w["wm_o"],
                   w["ln_g"][2:3], w["ln_b"][2:3])
    x5 = _ffn(x4.reshape(n, D_MODEL), w["f2_gu"], w["f2_d"], w["ln_g"][3:4], w["ln_b"][3:4])
    return x5.reshape(b, t, D_MODEL)


def _prompt_layer(x, mem_k, mem_v, layer, w):
    b, s, _ = x.shape
    n = b * s
    x2 = _ffn(x.reshape(n, D_MODEL), w["f1_gu"], w["f1_d"], w["ln_g"][0:1], w["ln_b"][0:1])
    tab = _rope_tables(jnp.arange(s, dtype=jnp.int32))
    (aq, akt, avt, aktb, avtb, bq, bkt, bvt, ikt, bktb, bvtb, iktb,
     iq, iw, cq, ck4, cv4, cktb, cvb) = _inproj_prompt(x2, w["in_pack"], tab, b, s)
    seq = lambda a: a.reshape(b, s, a.shape[-1])
    ya = _band_prompt(seq(aq), aktb, avtb, w["rel_bias"])
    yb = _dsa_prompt(seq(iq), seq(iw), seq(bq), iktb, bktb, bvtb)
    yc = _diff_prompt(seq(cq), cktb, seq(cvb), w["lam"], w["subln_g"], layer)
    y = _rest_of_layer(x2, ya, yb, yc, mem_k, mem_v, 0, b, s, w)
    keep = min(A_WINDOW, s)
    heads_last = lambda a: jnp.transpose(a[:, :, s - keep:].reshape(b, A_HEADS, HEAD_DIM, keep), (0, 3, 1, 2))
    rows_last = lambda a: jnp.swapaxes(a, 1, 2)
    state = (heads_last(akt), heads_last(avt), rows_last(bkt), rows_last(bvt), rows_last(ikt),
             ck4.reshape(b, s, C_HEADS, 2 * HEAD_DIM), cv4.reshape(b, s, C_HEADS, 2 * HEAD_DIM))
    return y, state


def _step_layer(x, p_len, past, mem_k, mem_v, layer, w):
    b, t, _ = x.shape
    first = layer * b
    n = b * t
    x2 = _ffn(x.reshape(n, D_MODEL), w["f1_gu"], w["f1_d"], w["ln_g"][0:1], w["ln_b"][0:1])
    tm = _row_tile(n, INPROJ_TM)
    assert tm % t == 0
    tab = jnp.tile(_rope_tables(p_len + jnp.arange(t, dtype=jnp.int32)), (1, 1, tm // t, 1))
    outs = _inproj_step(x2, w["in_pack"], tab)
    aq, ak, av, bq, bk, bv, iq, ik, iw, cq, ck, cv = [o.reshape(b, t, o.shape[-1]) for o in outs]
    pa_k, pa_v, pb_k, pb_v, pb_i, pc_k, pc_v = past
    ya = _band_step(aq, ak, av, pa_k, pa_v, w["rel_bias"], first)
    t_pad = -(-t // LANES) * LANES
    pad = lambda a: _pad_rows(a, t_pad)
    yb = _dsa_step(iq, iw, bq, pb_i, pb_k, pb_v, pad(ik), pad(bk), pad(bv), t, first)
    yc = _diff_step(cq, pc_k, pc_v, pad(ck), pad(cv), w["lam"], w["subln_g"], layer, t, first)
    y = _rest_of_layer(x2, ya, yb, yc, mem_k, mem_v, first, b, t, w)
    state = (ak.reshape(b, t, A_HEADS, HEAD_DIM), av.reshape(b, t, A_HEADS, HEAD_DIM), bk, bv, ik,
             ck.reshape(b, t, C_HEADS, 2 * HEAD_DIM), cv.reshape(b, t, C_HEADS, 2 * HEAD_DIM))
    return y, state


def _layer_weights(l, ln_g, ln_b, ffn1_w_gu, ffn1_w_d, ffn2_w_gu, ffn2_w_d, w_in, a_rel_bias, c_lambda,
                   c_subln_g, w_branch_a, w_branch_b, w_branch_c, w_out, w_mem_q, w_mem_k, w_mem_v, w_mem_o):
    wi = w_in[l]
    iw_end = IN_OFFS[9]
    gates_at = IN_OFFS[12]
    in_pack = jnp.concatenate(
        [wi[:, :iw_end], jnp.zeros((D_MODEL, IW_PAD), wi.dtype), wi[:, iw_end:gates_at]], axis=1)
    bf = lambda a: a.astype(BF16)
    return dict(
        ln_g=ln_g[l].astype(F32), ln_b=ln_b[l].astype(F32),
        f1_gu=bf(ffn1_w_gu[l]), f1_d=bf(ffn1_w_d[l]), f2_gu=bf(ffn2_w_gu[l]), f2_d=bf(ffn2_w_d[l]),
        in_pack=bf(in_pack), gates=bf(wi[:, gates_at:]),
        rel_bias=a_rel_bias[l], lam=c_lambda[l].astype(F32), subln_g=c_subln_g[l].astype(F32)[None, :],
        wb_a=bf(w_branch_a[l]), wb_b=bf(w_branch_b[l]), wb_c=bf(w_branch_c[l]), w_out=bf(w_out[l]),
        wm_q=bf(w_mem_q[l]), wm_k=bf(w_mem_k[l]), wm_v=bf(w_mem_v[l]), wm_o=bf(w_mem_o[l]),
    )


def kernel(x_prompt, x_sample, cache_a_k, cache_a_v, cache_b_k, cache_b_v, cache_b_idx, cache_c_k, cache_c_v, cache_mem_k, cache_mem_v, mem_prompt, ln_g, ln_b, ffn1_w_gu, ffn1_w_d, ffn2_w_gu, ffn2_w_d, w_in, a_rel_bias, c_lambda, c_subln_g, w_branch_a, w_branch_b, w_branch_c, w_out, w_mem_q, w_mem_k, w_mem_v, w_mem_o):
    bp, s, _ = x_prompt.shape
    bs, t, _ = x_sample.shape
    p_len = cache_b_k.shape[2]
    depth = ln_g.shape[0]
    yp, ys = x_prompt, x_sample
    st_p, st_s, mk_p, mv_p = [], [], [], []
    rows = depth * bs
    keys_last = lambda a: jnp.transpose(a, (0, 1, 3, 2)).reshape(rows, a.shape[3], a.shape[2])
    pairs = lambda a: a.reshape(rows, a.shape[2] * C_HEADS, 2 * HEAD_DIM)
    past = (cache_a_k.reshape(rows, -1, A_WIDTH), cache_a_v.reshape(rows, -1, A_WIDTH),
            keys_last(cache_b_k), keys_last(cache_b_v), keys_last(cache_b_idx), pairs(cache_c_k), pairs(cache_c_v))
    mem_k_s = cache_mem_k.reshape(rows, N_MEM, D_MODEL)
    mem_v_s = cache_mem_v.reshape(rows, N_MEM, D_MODEL)
    for l in range(depth):
        w = _layer_weights(l, ln_g, ln_b, ffn1_w_gu, ffn1_w_d, ffn2_w_gu, ffn2_w_d, w_in, a_rel_bias,
                           c_lambda, c_subln_g, w_branch_a, w_branch_b, w_branch_c, w_out,
                           w_mem_q, w_mem_k, w_mem_v, w_mem_o)
        mk, mv = _memproj(mem_prompt.reshape(bp * N_MEM, D_MODEL), w["wm_k"], w["wm_v"])
        mk = mk.reshape(bp, N_MEM, D_MODEL)
        mv = mv.reshape(bp, N_MEM, D_MODEL)
        yp, sp = _prompt_layer(yp, mk, mv, l, w)
        ys, ss = _step_layer(ys, p_len, past, mem_k_s, mem_v_s, l, w)
        st_p.append(sp)
        st_s.append(ss)
        mk_p.append(mk.reshape(bp, N_MEM, MEM_HEADS, MEM_HEAD_DIM))
        mv_p.append(mv.reshape(bp, N_MEM, MEM_HEADS, MEM_HEAD_DIM))
    stk = lambda sts: [jnp.stack([st[i] for st in sts]) for i in range(7)]
    (a_k_p, a_v_p, b_k_p, b_v_p, b_i_p, c_k_p, c_v_p) = stk(st_p)
    (a_k_s, a_v_s, b_k_s, b_v_s, b_i_s, c_k_s, c_v_s) = stk(st_s)
    return (yp, ys, a_k_p, a_v_p, b_k_p, b_v_p, b_i_p, c_k_p, c_v_p,
            jnp.stack(mk_p), jnp.stack(mv_p), a_k_s, a_v_s, b_k_s, b_v_s, b_i_s, c_k_s, c_v_s)
```

```python
import functools
import math

import jax
import jax.numpy as jnp
import numpy as np
from jax import lax
from jax.experimental import pallas as pl
from jax.experimental.pallas import tpu as pltpu

F32 = jnp.float32
BF16 = jnp.bfloat16

D_MODEL = 1024
CHUNK = 64
HEAD_DIM = 64
ROT_DIM = HEAD_DIM // 4
ROT_HALF = ROT_DIM // 2
ROPE_THETA = 500000.0
A_HEADS = 4
A_LEFT_CHUNKS = 8
A_WINDOW = A_LEFT_CHUNKS * CHUNK
REL_CLIP = 128
B_HEADS = 4
IDX_HEADS = 8
IDX_DIM = 64
TOPK_MAX = 256
C_HEADS = 4
N_MEM = 256
MEM_HEADS = 4
MEM_HEAD_DIM = D_MODEL // MEM_HEADS
D_FF = ((8 * D_MODEL // 3 + 127) // 128) * 128
N_BRANCH = 3
DEPTH = 2
ALPHA = (2.0 * DEPTH) ** 0.25
LN_EPS = 1e-5
SUBLN_EPS = 1e-5
A_WIDTH = A_HEADS * HEAD_DIM
B_WIDTH = B_HEADS * HEAD_DIM
C_WIDTH = C_HEADS * 2 * HEAD_DIM
IN_SIZES = (A_WIDTH, A_WIDTH, A_WIDTH,
            B_WIDTH, HEAD_DIM, HEAD_DIM, IDX_HEADS * IDX_DIM, IDX_DIM, IDX_HEADS,
            C_WIDTH, C_WIDTH, C_WIDTH,
            N_BRANCH * D_MODEL)
IN_OFFS = tuple(int(v) for v in np.cumsum((0,) + IN_SIZES))

LANES = 128
V7X_VMEM_LIMIT = 56 * 1024 * 1024

NEG = -1e30
LOG2E = math.log2(math.e)
INT_MIN = -2 ** 31

IW_PAD = LANES - HEAD_DIM - IDX_HEADS
P_AQ, P_AK, P_AV = 0, 256, 512
P_B = 768
P_I = 1152
P_C = 1792
P_END = 3328


def _dot(a, b):
    return jnp.dot(a, b, preferred_element_type=F32)


def _dot_nt(a, b):
    return lax.dot_general(a, b, (((1,), (1,)), ((), ())), preferred_element_type=F32)


def _layer_norm(v, g, b):
    mu = jnp.mean(v, axis=-1, keepdims=True)
    d = v - mu
    var = jnp.mean(d * d, axis=-1, keepdims=True)
    return d * lax.rsqrt(var + LN_EPS) * g + b


def _params(sem):
    return pltpu.CompilerParams(dimension_semantics=sem, vmem_limit_bytes=V7X_VMEM_LIMIT)


def _row_tile(n, want):
    t = min(n, want)
    assert n % t == 0
    return t


def _ffn_kernel(x_ref, wg_ref, wu_ref, wd_ref, g_ref, b_ref, o_ref, xb_ref, acc_ref):
    j = pl.program_id(1)

    @pl.when(j == 0)
    def _():
        xb_ref[...] = x_ref[...].astype(BF16)
        acc_ref[...] = jnp.zeros_like(acc_ref)

    xb = xb_ref[...]
    g = _dot(xb, wg_ref[...])
    u = _dot(xb, wu_ref[...])
    h = g * jax.nn.sigmoid(g) * u
    acc_ref[...] += _dot(h.astype(BF16), wd_ref[...])

    @pl.when(j == pl.num_programs(1) - 1)
    def _():
        o_ref[...] = _layer_norm(ALPHA * x_ref[...] + 0.5 * acc_ref[...], g_ref[...], b_ref[...])


def _ffn(x, w_gu, w_d, g, b):
    n = x.shape[0]
    tm = _row_tile(n, 1024)
    tf = 256
    nf = D_FF // tf
    return pl.pallas_call(
        _ffn_kernel,
        out_shape=jax.ShapeDtypeStruct((n, D_MODEL), F32),
        grid=(n // tm, nf),
        in_specs=[
            pl.BlockSpec((tm, D_MODEL), lambda i, j: (i, 0)),
            pl.BlockSpec((D_MODEL, tf), lambda i, j: (0, j)),
            pl.BlockSpec((D_MODEL, tf), lambda i, j: (0, j + nf)),
            pl.BlockSpec((tf, D_MODEL), lambda i, j: (j, 0)),
            pl.BlockSpec((1, D_MODEL), lambda i, j: (0, 0)),
            pl.BlockSpec((1, D_MODEL), lambda i, j: (0, 0)),
        ],
        out_specs=pl.BlockSpec((tm, D_MODEL), lambda i, j: (i, 0)),
        scratch_shapes=[pltpu.VMEM((tm, D_MODEL), BF16), pltpu.VMEM((tm, D_MODEL), F32)],
        compiler_params=_params(("parallel", "arbitrary")),
        name="ffn",
    )(x, w_gu, w_gu, w_d, g, b)


def _rope_tables(pos):
    t = pos.shape[0]
    inv_freq = ROPE_THETA ** (-jnp.arange(ROT_HALF, dtype=F32) / ROT_HALF)
    ang = pos.astype(F32)[:, None] * inv_freq
    cos, sin = jnp.cos(ang), jnp.sin(ang)
    rest = HEAD_DIM - ROT_DIM
    ones = lambda w: jnp.ones((t, w), F32)
    zeros = lambda w: jnp.zeros((t, w), F32)
    c64 = jnp.concatenate([cos, cos, ones(rest)], axis=1)
    lo64 = jnp.concatenate([-sin, zeros(ROT_HALF), zeros(rest)], axis=1)
    hi64 = jnp.concatenate([zeros(ROT_HALF), sin, zeros(rest)], axis=1)
    iw64 = jnp.concatenate([jnp.full((t, IDX_HEADS), IDX_HEADS ** -0.5, F32), ones(HEAD_DIM - IDX_HEADS)], axis=1)
    cat = lambda a, b_: jnp.concatenate([a, b_], axis=1)
    both = jnp.stack([cat(c64, c64), cat(lo64, lo64), cat(hi64, hi64)])
    upper_id = jnp.stack([cat(c64, ones(HEAD_DIM)), cat(lo64, zeros(HEAD_DIM)), cat(hi64, zeros(HEAD_DIM))])
    upper_iw = jnp.stack([cat(c64, iw64), cat(lo64, zeros(HEAD_DIM)), cat(hi64, zeros(HEAD_DIM))])
    return jnp.stack([both, upper_id, upper_iw])


def _rope_apply(z, tab_ref, variant):
    cos = tab_ref[variant, 0]
    lo = tab_ref[variant, 1]
    hi = tab_ref[variant, 2]
    outs = []
    for k in range(z.shape[1] // LANES):
        zc = z[:, k * LANES:(k + 1) * LANES]
        outs.append(zc * cos + pltpu.roll(zc, LANES - ROT_HALF, 1) * lo + pltpu.roll(zc, ROT_HALF, 1) * hi)
    return outs[0] if len(outs) == 1 else jnp.concatenate(outs, axis=1)


def _inproj_sections(x_ref, w_ref, tab_ref):
    xb = x_ref[...].astype(BF16)
    za = _dot(xb, w_ref[:, P_AQ:P_B])
    zb = _dot(xb, w_ref[:, P_B:P_I])
    zi = _dot(xb, w_ref[:, P_I:P_C])
    zc = _dot(xb, w_ref[:, P_C:P_END])
    qs = HEAD_DIM ** -0.5 * LOG2E
    return dict(
        aq=za[:, 0:256] * qs, ak=za[:, 256:512], av=za[:, 512:768],
        bq=_rope_apply(zb[:, 0:256], tab_ref, 0) * qs,
        bkv=_rope_apply(zb[:, 256:384], tab_ref, 1),
        iq=_rope_apply(zi[:, 0:512], tab_ref, 0),
        ikw=_rope_apply(zi[:, 512:640], tab_ref, 2),
        cq=_rope_apply(zc[:, 0:512], tab_ref, 0) * qs,
        ck=_rope_apply(zc[:, 512:1024], tab_ref, 0),
        cv=zc[:, 1024:1536])


def _store_heads_interleaved(o_ref, v, heads):
    rows = v.shape[0]
    for h in range(heads):
        o_ref[pl.ds(h, rows, stride=heads), :] = v[:, h * LANES:(h + 1) * LANES]


def _inproj_prompt_kernel(x_ref, w_ref, tab_ref,
                          aq_ref, akt_ref, avt_ref, aktb_ref, avtb_ref,
                          bqt_ref, bkt_ref, bvt_ref, ikt_ref, bkb_ref, bvtb_ref, ikb_ref,
                          iqt_ref, iwt_ref, cq_ref, ck4_ref, cv4_ref, cktb_ref, cvb_ref):
    s = _inproj_sections(x_ref, w_ref, tab_ref)
    aq_ref[...] = s["aq"].astype(BF16)
    akt = s["ak"].T
    avt = s["av"].T
    akt_ref[0] = akt
    avt_ref[0] = avt
    aktb_ref[0] = akt.astype(BF16)
    avtb_ref[0] = avt.astype(BF16)
    bqt_ref[0] = s["bq"].T.astype(BF16)
    kvt = s["bkv"].T
    bkt_ref[0] = kvt[0:HEAD_DIM]
    bvt_ref[0] = kvt[HEAD_DIM:2 * HEAD_DIM]
    bkb_ref[...] = s["bkv"][:, 0:HEAD_DIM].astype(BF16)
    bvtb_ref[0] = kvt[HEAD_DIM:2 * HEAD_DIM].astype(BF16)
    kwt = s["ikw"].T
    ikt_ref[0] = kwt[0:IDX_DIM]
    ikb_ref[...] = s["ikw"][:, 0:IDX_DIM].astype(BF16)
    iqt_ref[0] = s["iq"].T.astype(BF16)
    iwt_ref[0] = kwt[IDX_DIM:IDX_DIM + IDX_HEADS]
    cq_ref[...] = s["cq"].astype(BF16)
    _store_heads_interleaved(ck4_ref, s["ck"], C_HEADS)
    _store_heads_interleaved(cv4_ref, s["cv"], C_HEADS)
    cktb_ref[0] = s["ck"].T.astype(BF16)
    cvb_ref[...] = s["cv"].astype(BF16)


INPROJ_TM = 256


def _inproj_prompt(x, w_pack, tab, bsz, seq):
    n = x.shape[0]
    tm = INPROJ_TM
    spb = seq // tm
    tok = lambda w: pl.BlockSpec((tm, w), lambda i: (i, 0))
    tr = lambda d: pl.BlockSpec((1, d, tm), lambda i: (i // spb, 0, i % spb))
    sds = jax.ShapeDtypeStruct
    tok_o = lambda w, dt: (sds((n, w), dt), tok(w))
    tr_o = lambda d, dt: (sds((bsz, d, seq), dt), tr(d))
    il_o = (sds((n * C_HEADS, LANES), F32), pl.BlockSpec((tm * C_HEADS, LANES), lambda i: (i, 0)))
    outs = [tok_o(A_WIDTH, BF16), tr_o(A_WIDTH, F32), tr_o(A_WIDTH, F32), tr_o(A_WIDTH, BF16), tr_o(A_WIDTH, BF16),
            tr_o(B_WIDTH, BF16), tr_o(HEAD_DIM, F32), tr_o(HEAD_DIM, F32), tr_o(IDX_DIM, F32),
            tok_o(HEAD_DIM, BF16), tr_o(HEAD_DIM, BF16), tok_o(IDX_DIM, BF16),
            tr_o(IDX_HEADS * IDX_DIM, BF16), tr_o(IDX_HEADS, F32), tok_o(C_WIDTH, BF16),
            il_o, il_o, tr_o(C_WIDTH, BF16), tok_o(C_WIDTH, BF16)]
    return pl.pallas_call(
        _inproj_prompt_kernel,
        out_shape=tuple(o[0] for o in outs),
        grid=(n // tm,),
        in_specs=[tok(D_MODEL),
                  pl.BlockSpec((D_MODEL, P_END), lambda i: (0, 0)),
                  pl.BlockSpec((3, 3, tm, LANES), lambda i: (0, 0, i % spb, 0))],
        out_specs=tuple(o[1] for o in outs),
        compiler_params=_params(("parallel",)),
        name="inproj_prompt",
    )(x, w_pack, tab)


def _inproj_step_kernel(x_ref, w_ref, tab_ref,
                        aq_ref, ak_ref, av_ref, bq_ref, bk_ref, bv_ref,
                        iq_ref, ik_ref, iw_ref, cq_ref, ck_ref, cv_ref):
    s = _inproj_sections(x_ref, w_ref, tab_ref)
    aq_ref[...] = s["aq"].astype(BF16)
    ak_ref[...] = s["ak"]
    av_ref[...] = s["av"]
    bq_ref[...] = s["bq"].astype(BF16)
    bk_ref[...] = s["bkv"][:, 0:HEAD_DIM]
    bv_ref[...] = s["bkv"][:, HEAD_DIM:2 * HEAD_DIM]
    iq_ref[...] = s["iq"].astype(BF16)
    ik_ref[...] = s["ikw"][:, 0:IDX_DIM]
    iw_ref[...] = s["ikw"][:, IDX_DIM:IDX_DIM + IDX_HEADS]
    cq_ref[...] = s["cq"].astype(BF16)
    ck_ref[...] = s["ck"]
    cv_ref[...] = s["cv"]


def _inproj_step(x, w_pack, tab):
    n = x.shape[0]
    tm = _row_tile(n, INPROJ_TM)
    widths = ((A_WIDTH, BF16), (A_WIDTH, F32), (A_WIDTH, F32), (B_WIDTH, BF16), (HEAD_DIM, F32), (HEAD_DIM, F32),
              (IDX_HEADS * IDX_DIM, BF16), (IDX_DIM, F32), (IDX_HEADS, F32),
              (C_WIDTH, BF16), (C_WIDTH, F32), (C_WIDTH, F32))
    return pl.pallas_call(
        _inproj_step_kernel,
        out_shape=tuple(jax.ShapeDtypeStruct((n, w), dt) for w, dt in widths),
        grid=(n // tm,),
        in_specs=[
            pl.BlockSpec((tm, D_MODEL), lambda i: (i, 0)),
            pl.BlockSpec((D_MODEL, P_END), lambda i: (0, 0)),
            pl.BlockSpec((3, 3, tm, LANES), lambda i: (0, 0, 0, 0)),
        ],
        out_specs=tuple(pl.BlockSpec((tm, w), lambda i: (i, 0)) for w, _ in widths),
        compiler_params=_params(("parallel",)),
        name="inproj_step",
    )(x, w_pack, tab)


def _qk(q, piece):
    kind, k = piece
    return _dot(q, k) if kind == "T" else _dot_nt(q, k)


def _pv(p, piece):
    kind, v = piece
    return _dot_nt(p, v) if kind == "T" else _dot(p, v)


def _piece_len(piece):
    kind, a = piece
    return a.shape[1] if kind == "T" else a.shape[0]


def _cat(parts):
    return parts[0] if len(parts) == 1 else jnp.concatenate(parts, axis=1)


def _pv_pieces(p, pieces):
    off = 0
    o = None
    for piece in pieces:
        n = _piece_len(piece)
        t = _pv(p[:, off:off + n], piece)
        o = t if o is None else o + t
        off += n
    return o


def _toeplitz_bias(rel_bias, q_rows, k_cols, q_off):
    period = pl.next_power_of_2(q_rows + k_cols)
    e = np.arange(period)
    e = np.where(e >= k_cols, e - period, e)
    idx = np.clip(q_off - e, -REL_CLIP, REL_CLIP) + REL_CLIP
    table = rel_bias[:, idx].astype(F32) * LOG2E
    flat = jnp.tile(table, (1, q_rows + 1))[:, :q_rows * (period - 1)]
    return flat.reshape(rel_bias.shape[0], q_rows, period - 1)[:, :, :k_cols]


def _band_attend(q, pieces, bias_ref, o_ref):
    for h in range(A_HEADS):
        hs = slice(h * HEAD_DIM, (h + 1) * HEAD_DIM)
        qh = q[:, hs]
        ss = []
        off = 0
        for k_of, _, extra in pieces:
            kp = k_of(h)
            n = _piece_len(kp)
            s = _qk(qh, kp) + bias_ref[h, :, off:off + n]
            if extra is not None:
                s = s + extra
            ss.append(s)
            off += n
        m = ss[0].max(axis=-1, keepdims=True)
        for s in ss[1:]:
            m = jnp.maximum(m, s.max(axis=-1, keepdims=True))
        l = jnp.zeros_like(m)
        o = jnp.zeros((q.shape[0], HEAD_DIM), F32)
        for s, (_, v_of, _) in zip(ss, pieces):
            p = jnp.exp2(s - m)
            l = l + p.sum(axis=-1, keepdims=True)
            o = o + _pv(p.astype(BF16), v_of(h))
        o_ref[0, :, hs] = (o / l).astype(o_ref.dtype)


BAND_QB = 256
BAND_KB = 3


def _band_prompt_kernel(q_ref, k0_ref, k1_ref, k2_ref, v0_ref, v1_ref, v2_ref, bias_ref, o_ref):
    j = pl.program_id(1)
    pieces = []
    for kb, (k_ref, v_ref) in enumerate(((k0_ref, v0_ref), (k1_ref, v1_ref), (k2_ref, v2_ref))):
        back = BAND_KB - 1 - kb
        extra = None if back == 0 else jnp.where(j >= back, 0.0, NEG)
        head_rows = lambda ref: (lambda h: ("T", ref[0, h * HEAD_DIM:(h + 1) * HEAD_DIM, :]))
        pieces.append((head_rows(k_ref), head_rows(v_ref), extra))
    _band_attend(q_ref[0], pieces, bias_ref, o_ref)


def _band_prompt(aq, akt, avt, rel_bias):
    b, s, _ = aq.shape
    assert A_WINDOW == (BAND_KB - 1) * BAND_QB and s % BAND_QB == 0
    kw = BAND_KB * BAND_QB
    r = np.arange(BAND_QB)[:, None] + A_WINDOW
    c = np.arange(kw)[None, :]
    in_band = (c // CHUNK <= r // CHUNK) & (c // CHUNK >= r // CHUNK - A_LEFT_CHUNKS)
    bias = jnp.where(in_band[None], _toeplitz_bias(rel_bias, BAND_QB, kw, A_WINDOW), NEG)
    qspec = pl.BlockSpec((1, BAND_QB, A_WIDTH), lambda bi, j: (bi, j, 0))
    kspecs = [pl.BlockSpec((1, A_WIDTH, BAND_QB),
                           functools.partial(lambda bi, j, back: (bi, 0, jnp.maximum(j - back, 0)),
                                             back=BAND_KB - 1 - kb))
              for kb in range(BAND_KB)]
    return pl.pallas_call(
        _band_prompt_kernel,
        out_shape=jax.ShapeDtypeStruct((b, s, A_WIDTH), BF16),
        grid=(b, s // BAND_QB),
        in_specs=[qspec] + kspecs + kspecs + [pl.BlockSpec((A_HEADS, BAND_QB, kw), lambda bi, j: (0, 0, 0))],
        out_specs=qspec,
        compiler_params=_params(("parallel", "parallel")),
        name="band_prompt",
    )(aq, akt, akt, akt, avt, avt, avt, bias)


def _band_step_kernel(q_ref, pk_ref, k_ref, pv_ref, v_ref, bias_ref, o_ref):
    head_cols = lambda a: (lambda h: ("N", a[:, h * HEAD_DIM:(h + 1) * HEAD_DIM]))
    pieces = [(head_cols(pk_ref[0].astype(BF16)), head_cols(pv_ref[0].astype(BF16)), None),
              (head_cols(k_ref[0].astype(BF16)), head_cols(v_ref[0].astype(BF16)), None)]
    _band_attend(q_ref[0], pieces, bias_ref, o_ref)


def _band_step(aq, ak, av, pk, pv, rel_bias, first):
    b, t, _ = aq.shape
    w = pk.shape[1]
    bias = _toeplitz_bias(rel_bias, t, w + t, w)
    new = pl.BlockSpec((1, t, A_WIDTH), lambda bi: (bi, 0, 0))
    old = pl.BlockSpec((1, w, A_WIDTH), lambda bi: (first + bi, 0, 0))
    return pl.pallas_call(
        _band_step_kernel,
        out_shape=jax.ShapeDtypeStruct((b, t, A_WIDTH), BF16),
        grid=(b,),
        in_specs=[new, old, new, old, new, pl.BlockSpec((A_HEADS, t, w + t), lambda bi: (0, 0, 0))],
        out_specs=new,
        compiler_params=_params(("parallel",)),
        name="band_step",
    )(aq, pk, ak, pv, av, bias)


def _admissible(q0, q_rows, k0, k_cols, k_true):
    kpos = lax.broadcasted_iota(jnp.int32, (q_rows, k_cols), 1) + k0
    qpos = lax.broadcasted_iota(jnp.int32, (q_rows, k_cols), 0) + q0
    shift = CHUNK.bit_length() - 1
    return ((kpos >> shift) <= (qpos >> shift)) & (kpos < k_true)


def _free_cols(q0, k_cols, k_true):
    return min((q0 // CHUNK + 1) * CHUNK, k_true, k_cols) // LANES * LANES


def _count_ge(ref, cand16, width):
    acc = None
    for t in range(width // LANES):
        one = jnp.where(ref[:, t * LANES:(t + 1) * LANES] >= cand16, jnp.int16(1), jnp.int16(0))
        acc = one if acc is None else acc + one
    return jnp.sum(acc.astype(F32), axis=1, keepdims=True)


def _search16(ref, want, width):
    def body(it, prefix):
        cand = prefix | jnp.left_shift(jnp.int32(1), 15 - it)
        cnt = _count_ge(ref, (cand - 32768).astype(jnp.int16), width)
        return jnp.where(cnt >= want, cand, prefix)
    return lax.fori_loop(0, 16, body, jnp.zeros((ref.shape[0], 1), jnp.int32))


def _select_topk(score, topk, hi_ref, lo_ref):
    q_rows, l = score.shape
    bits = lax.bitcast_convert_type(score, jnp.int32)
    key = bits ^ ((bits >> 31) & jnp.int32(0x7FFFFFFF))
    kf = jnp.float32(topk)
    hi_ref[...] = (key >> 16).astype(jnp.int16)
    hi_thr = _search16(hi_ref, kf, l) - 32768
    hi_thr16 = hi_thr.astype(jnp.int16)
    above = _count_ge(hi_ref, hi_thr16 + jnp.int16(1), l) * jnp.where(hi_thr < 32767, 1.0, 0.0)
    low = ((key & jnp.int32(0xFFFF)) - 32768).astype(jnp.int16)
    lo_ref[...] = jnp.where(hi_ref[...] == hi_thr16, low, jnp.int16(-32768))
    lo_thr = _search16(lo_ref, kf - above, l)
    thr = (hi_thr << 16) | lo_thr
    need = kf - jnp.sum(jnp.where(key > thr, 1.0, 0.0), axis=1, keepdims=True)
    rr = lax.broadcasted_iota(jnp.int32, (LANES, LANES), 0)
    cc = lax.broadcasted_iota(jnp.int32, (LANES, LANES), 1)
    before = jnp.where(rr < cc, 1.0, 0.0).astype(BF16)
    carry = jnp.zeros((q_rows, 1), F32)
    sel = []
    for t in range(l // LANES):
        key_t = key[:, t * LANES:(t + 1) * LANES]
        eq_t = jnp.where(key_t == thr, 1.0, 0.0)
        rank = carry + _dot(eq_t.astype(BF16), before)
        take_tie = jnp.where(rank < need, eq_t, 0.0)
        sel.append(jnp.where(key_t > thr, 1.0, take_tie))
        carry = carry + jnp.sum(eq_t, axis=1, keepdims=True)
    return jnp.concatenate(sel, axis=1)


def _dsa_attend(q0, k_true, topk, iq, iw, bq, ik_pieces, bk_pieces, bv_pieces, o_ref, hi_ref, lo_ref):
    qn = iq.shape[0]
    l = sum(_piece_len(p) for p in ik_pieces)
    score = None
    for h in range(IDX_HEADS):
        hs = slice(h * IDX_DIM, (h + 1) * IDX_DIM)
        dots = _cat([_qk(iq[:, hs], p) for p in ik_pieces])
        term = iw[:, h:h + 1] * jnp.maximum(dots, 0.0)
        score = term if score is None else score + term
    free = _free_cols(q0, l, k_true)
    adm = _admissible(q0, qn, free, l - free, k_true)
    if k_true <= topk:
        tail = jnp.where(adm, 0.0, NEG)
        mask = tail if free == 0 else jnp.concatenate([jnp.zeros((qn, free), F32), tail], axis=1)
    else:
        score = score + 0.0
        tail = jnp.where(adm, score[:, free:], -jnp.inf)
        score = tail if free == 0 else jnp.concatenate([score[:, :free], tail], axis=1)
        sel = _select_topk(score, topk, hi_ref, lo_ref)
        tail = jnp.where(adm, jnp.where(sel[:, free:] > 0.0, 0.0, NEG), NEG)
        mask = tail if free == 0 else jnp.concatenate([jnp.where(sel[:, :free] > 0.0, 0.0, NEG), tail], axis=1)
    for h in range(B_HEADS):
        hs = slice(h * HEAD_DIM, (h + 1) * HEAD_DIM)
        s = _cat([_qk(bq[:, hs], p) for p in bk_pieces]) + mask
        m = s.max(axis=-1, keepdims=True)
        p = jnp.exp2(s - m)
        den = p.sum(axis=-1, keepdims=True)
        o = _pv_pieces(p.astype(BF16), bv_pieces)
        o_ref[0, :, hs] = (o / den).astype(o_ref.dtype)


def _bit_planes(key):
    a = [key[8 * j:8 * (j + 1), :] for j in range(32)]
    j, m = 16, 0x0000FFFF
    while j:
        for k in range(32):
            if k & j:
                continue
            t = (lax.shift_right_logical(a[k], jnp.int32(j)) ^ a[k | j]) & jnp.int32(m - (1 << 32) if m >> 31 else m)
            a[k] = a[k] ^ (t << j)
            a[k | j] = a[k | j] ^ t
        j >>= 1
        m = (m ^ (m << j)) & 0xFFFFFFFF
    return a


BITS_GROUP = 256


def _kth_largest_t(key, topk):
    l, qn = key.shape
    groups = l // BITS_GROUP
    ukey = key ^ jnp.int32(INT_MIN)
    planes = [_bit_planes(ukey[g * BITS_GROUP:(g + 1) * BITS_GROUP, :]) for g in range(groups)]
    alive = [jnp.full((8, qn), -1, jnp.int32) for _ in range(groups)]
    above = jnp.zeros((1, qn), F32)
    thr = jnp.zeros((1, qn), jnp.int32)
    kf = jnp.float32(topk)
    for b in range(31, -1, -1):
        with_bit = [alive[g] & planes[g][b] for g in range(groups)]
        cnt = lax.population_count(with_bit[0])
        for g in range(1, groups):
            cnt = cnt + lax.population_count(with_bit[g])
        c = above + jnp.sum(cnt.astype(F32), axis=0, keepdims=True)
        take = c >= kf
        thr = jnp.where(take, thr | jnp.int32((1 << b) - (1 << 32) if b == 31 else (1 << b)), thr)
        above = jnp.where(take, above, c)
        alive = [jnp.where(take, with_bit[g], alive[g] ^ with_bit[g]) for g in range(groups)]
    return thr ^ jnp.int32(INT_MIN), above


def _select_topk_t(score, topk):
    l, qn = score.shape
    bits = lax.bitcast_convert_type(score, jnp.int32)
    key = bits ^ ((bits >> 31) & jnp.int32(0x7FFFFFFF))
    thr, above = _kth_largest_t(key, topk)
    need = jnp.float32(topk) - above
    rr = lax.broadcasted_iota(jnp.int32, (LANES, LANES), 0)
    cc = lax.broadcasted_iota(jnp.int32, (LANES, LANES), 1)
    earlier = jnp.where(cc < rr, 1.0, 0.0).astype(BF16)
    carry = jnp.zeros((1, qn), F32)
    sel = []
    for t in range(l // LANES):
        key_t = key[t * LANES:(t + 1) * LANES, :]
        eq_t = jnp.where(key_t == thr, 1.0, 0.0)
        rank = carry + _dot(earlier, eq_t.astype(BF16))
        take_tie = jnp.where(rank < need, eq_t, 0.0)
        sel.append(jnp.where(key_t > thr, 1.0, take_tie))
        carry = carry + jnp.sum(eq_t, axis=0, keepdims=True)
    return jnp.concatenate(sel, axis=0)


def _dsa_prompt_kernel(q0, k_len, topk, iqt_ref, iwt_ref, bqt_ref, ik_ref, bk_ref, bvt_ref, o_ref):
    qn = iqt_ref.shape[2]
    ik = ik_ref[0]
    score = None
    for h in range(IDX_HEADS):
        dots = _dot(ik, iqt_ref[0, h * IDX_DIM:(h + 1) * IDX_DIM, :])
        term = iwt_ref[0, h:h + 1, :] * jnp.maximum(dots, 0.0)
        score = term if score is None else score + term
    free = _free_cols(q0, k_len, k_len)
    kpos = lax.broadcasted_iota(jnp.int32, (k_len - free, qn), 0) + free
    qpos = lax.broadcasted_iota(jnp.int32, (k_len - free, qn), 1) + q0
    shift = CHUNK.bit_length() - 1
    adm = (kpos >> shift) <= (qpos >> shift)
    if k_len <= topk:
        tail = jnp.where(adm, 0.0, NEG)
        mask = tail if free == 0 else jnp.concatenate([jnp.zeros((free, qn), F32), tail], axis=0)
    else:
        score = score + 0.0
        tail = jnp.where(adm, score[free:, :], -jnp.inf)
        score = tail if free == 0 else jnp.concatenate([score[:free, :], tail], axis=0)
        sel = _select_topk_t(score, topk)
        tail = jnp.where(adm, jnp.where(sel[free:, :] > 0.0, 0.0, NEG), NEG)
        mask = tail if free == 0 else jnp.concatenate([jnp.where(sel[:free, :] > 0.0, 0.0, NEG), tail], axis=0)
    bk = bk_ref[0]
    outs = []
    for h in range(B_HEADS):
        s = _dot(bk, bqt_ref[0, h * HEAD_DIM:(h + 1) * HEAD_DIM, :]) + mask
        m = s.max(axis=0, keepdims=True)
        p = jnp.exp2(s - m)
        den = p.sum(axis=0, keepdims=True)
        outs.append(_dot(bvt_ref[0], p.astype(BF16)) / den)
    o_ref[0] = jnp.concatenate(outs, axis=0).T.astype(o_ref.dtype)


CAUSAL_QB = 256


def _dsa_prompt(iqt, iwt, bqt, ikb, bkb, bvtb):
    b, _, s = iqt.shape
    qb = CAUSAL_QB
    topk = min(TOPK_MAX, s // 4)
    outs = []
    for k in range(s // qb):
        k_len = (k + 1) * qb
        qspec = lambda d, k=k: pl.BlockSpec((1, d, qb), lambda bi: (bi, 0, k))
        rows = pl.BlockSpec((1, k_len, HEAD_DIM), lambda bi: (bi, 0, 0))
        outs.append(pl.pallas_call(
            functools.partial(_dsa_prompt_kernel, k * qb, k_len, topk),
            out_shape=jax.ShapeDtypeStruct((b, qb, B_WIDTH), BF16),
            grid=(b,),
            in_specs=[qspec(IDX_HEADS * IDX_DIM), qspec(IDX_HEADS), qspec(B_WIDTH), rows, rows,
                      pl.BlockSpec((1, HEAD_DIM, k_len), lambda bi: (bi, 0, 0))],
            out_specs=pl.BlockSpec((1, qb, B_WIDTH), lambda bi: (bi, 0, 0)),
            compiler_params=_params(("parallel",)),
            name="dsa_prompt",
        )(iqt, iwt, bqt, ikb, bkb, bvtb))
    return jnp.concatenate(outs, axis=1)


def _dsa_step_kernel(q0, k_true, topk, iq_ref, iw_ref, bq_ref, pik_ref, pbk_ref, pbv_ref,
                     ik_ref, bk_ref, bv_ref, o_ref, hi_ref, lo_ref):
    two = lambda old, new: [("T", old[0].astype(BF16)), ("N", new[0].astype(BF16))]
    _dsa_attend(q0, k_true, topk, iq_ref[0], iw_ref[0], bq_ref[0],
                two(pik_ref, ik_ref), two(pbk_ref, bk_ref), two(pbv_ref, bv_ref), o_ref, hi_ref, lo_ref)


def _dsa_step(iq, iw, bq, pik, pbk, pbv, ik, bk, bv, t_true, first):
    b, t, _ = iq.shape
    p_len = pik.shape[2]
    assert p_len % LANES == 0 and ik.shape[1] % LANES == 0
    k_true = p_len + t_true
    topk = min(TOPK_MAX, k_true // 4)
    qspec = lambda w: pl.BlockSpec((1, t, w), lambda bi: (bi, 0, 0))
    old = pl.BlockSpec((1, HEAD_DIM, p_len), lambda bi: (first + bi, 0, 0))
    new = pl.BlockSpec((1, ik.shape[1], HEAD_DIM), lambda bi: (bi, 0, 0))
    return pl.pallas_call(
        functools.partial(_dsa_step_kernel, p_len, k_true, topk),
        out_shape=jax.ShapeDtypeStruct((b, t, B_WIDTH), BF16),
        grid=(b,),
        in_specs=[qspec(IDX_HEADS * IDX_DIM), qspec(IDX_HEADS), qspec(B_WIDTH), old, old, old, new, new, new],
        out_specs=qspec(B_WIDTH),
        scratch_shapes=[pltpu.VMEM((t, p_len + ik.shape[1]), jnp.int16)] * 2,
        compiler_params=_params(("parallel",)),
        name="dsa_step",
    )(iq, iw, bq, pik, pbk, pbv, ik, bk, bv)


def _diff_attend(q0, k_true, lam_init, q, k_of, v_of, lam_ref, g_ref, o_ref):
    qn = q.shape[0]
    lp = lam_ref[...]
    lam = (jnp.exp(jnp.sum(lp[0:1] * lp[1:2], axis=1, keepdims=True))
           - jnp.exp(jnp.sum(lp[2:3] * lp[3:4], axis=1, keepdims=True)) + lam_init)
    l = sum(_piece_len(p) for p in k_of(0, 0))
    free = _free_cols(q0, l, k_true)
    tail = jnp.where(_admissible(q0, qn, free, l - free, k_true), 0.0, NEG)
    gain = g_ref[...] * (1.0 - lam_init)
    for h in range(C_HEADS):
        es, invs = [], []
        for part in range(2):
            cs = slice((2 * h + part) * HEAD_DIM, (2 * h + part + 1) * HEAD_DIM)
            s = _cat([_qk(q[:, cs], p) for p in k_of(h, part)])
            s = s + tail if free == 0 else jnp.concatenate([s[:, :free], s[:, free:] + tail], axis=1)
            m = s.max(axis=-1, keepdims=True)
            e = jnp.exp2(s - m)
            es.append(e)
            invs.append(1.0 / e.sum(axis=-1, keepdims=True))
        a = es[0] * invs[0] - es[1] * (lam * invs[1])
        o = _pv_pieces(a.astype(BF16), v_of(h))
        o = o * lax.rsqrt(jnp.mean(o * o, axis=-1, keepdims=True) + SUBLN_EPS) * gain
        o_ref[0, :, h * 2 * HEAD_DIM:(h + 1) * 2 * HEAD_DIM] = o.astype(o_ref.dtype)


def _diff_prompt_kernel(q0, k_len, lam_init, q_ref, kt_ref, v_ref, lam_ref, g_ref, o_ref):
    k_of = lambda h, part: [("T", kt_ref[0, (2 * h + part) * HEAD_DIM:(2 * h + part + 1) * HEAD_DIM, :])]
    v_of = lambda h: [("N", v_ref[0, :, h * 2 * HEAD_DIM:(h + 1) * 2 * HEAD_DIM])]
    _diff_attend(q0, k_len, lam_init, q_ref[0], k_of, v_of, lam_ref, g_ref, o_ref)


def _lam_init(layer):
    return 0.8 - 0.6 * math.exp(-0.3 * layer)


def _diff_prompt(cq, cktb, cvb, lam_p, subln_g, layer):
    b, s, _ = cq.shape
    qb = CAUSAL_QB
    small = [pl.BlockSpec((4, HEAD_DIM), lambda bi: (0, 0)), pl.BlockSpec((1, 2 * HEAD_DIM), lambda bi: (0, 0))]
    outs = []
    for k in range(s // qb):
        k_len = (k + 1) * qb
        outs.append(pl.pallas_call(
            functools.partial(_diff_prompt_kernel, k * qb, k_len, _lam_init(layer)),
            out_shape=jax.ShapeDtypeStruct((b, qb, C_WIDTH), BF16),
            grid=(b,),
            in_specs=[pl.BlockSpec((1, qb, C_WIDTH), functools.partial(lambda bi, k: (bi, k, 0), k=k)),
                      pl.BlockSpec((1, C_WIDTH, k_len), lambda bi: (bi, 0, 0)),
                      pl.BlockSpec((1, k_len, C_WIDTH), lambda bi: (bi, 0, 0))] + small,
            out_specs=pl.BlockSpec((1, qb, C_WIDTH), lambda bi: (bi, 0, 0)),
            compiler_params=_params(("parallel",)),
            name="diff_prompt",
        )(cq, cktb, cvb, lam_p, subln_g))
    return jnp.concatenate(outs, axis=1)


def _diff_step_kernel(q0, k_true, lam_init, q_ref, pk_ref, pv_ref, k_ref, v_ref, lam_ref, g_ref, o_ref):
    p_len = pk_ref.shape[1] // C_HEADS
    kn = k_ref[0].astype(BF16)
    vn = v_ref[0].astype(BF16)

    def k_of(h, part):
        old = pk_ref[0, pl.ds(h, p_len, stride=C_HEADS), :].astype(BF16)
        cs = slice((2 * h + part) * HEAD_DIM, (2 * h + part + 1) * HEAD_DIM)
        return [("N", old[:, part * HEAD_DIM:(part + 1) * HEAD_DIM]), ("N", kn[:, cs])]

    def v_of(h):
        old = pv_ref[0, pl.ds(h, p_len, stride=C_HEADS), :].astype(BF16)
        return [("N", old), ("N", vn[:, h * 2 * HEAD_DIM:(h + 1) * 2 * HEAD_DIM])]

    _diff_attend(q0, k_true, lam_init, q_ref[0], k_of, v_of, lam_ref, g_ref, o_ref)


def _diff_step(cq, pck, pcv, ck, cv, lam_p, subln_g, layer, t_true, first):
    b, t, _ = cq.shape
    p_len = pck.shape[1] // C_HEADS
    assert p_len % LANES == 0 and ck.shape[1] % LANES == 0
    qspec = pl.BlockSpec((1, t, C_WIDTH), lambda bi: (bi, 0, 0))
    old = pl.BlockSpec((1, p_len * C_HEADS, LANES), lambda bi: (first + bi, 0, 0))
    new = pl.BlockSpec((1, ck.shape[1], C_WIDTH), lambda bi: (bi, 0, 0))
    return pl.pallas_call(
        functools.partial(_diff_step_kernel, p_len, p_len + t_true, _lam_init(layer)),
        out_shape=jax.ShapeDtypeStruct((b, t, C_WIDTH), BF16),
        grid=(b,),
        in_specs=[qspec, old, old, new, new,
                  pl.BlockSpec((4, HEAD_DIM), lambda bi: (0, 0)), pl.BlockSpec((1, 2 * HEAD_DIM), lambda bi: (0, 0))],
        out_specs=qspec,
        compiler_params=_params(("parallel",)),
        name="diff_step",
    )(cq, pck, pcv, ck, cv, lam_p, subln_g)


def _merge_kernel(x_ref, ya_ref, yb_ref, yc_ref, wg_ref, wa_ref, wb_ref, wc_ref, wo_ref, g_ref, b_ref, o_ref):
    x = x_ref[...]
    xb = x.astype(BF16)
    merged = None
    for k, (y_ref, w_ref) in enumerate(((ya_ref, wa_ref), (yb_ref, wb_ref), (yc_ref, wc_ref))):
        gate = jax.nn.sigmoid(_dot(xb, wg_ref[:, k * D_MODEL:(k + 1) * D_MODEL]))
        term = gate * _dot(y_ref[...], w_ref[...])
        merged = term if merged is None else merged + term
    out = _dot(merged.astype(BF16), wo_ref[...])
    o_ref[...] = _layer_norm(ALPHA * x + out, g_ref[...], b_ref[...])


def _merge(x, ya, yb, yc, w_gates, wb_a, wb_b, wb_c, w_out, g, b):
    n = x.shape[0]
    tm = _row_tile(n, 512)
    row = lambda w: pl.BlockSpec((tm, w), lambda i: (i, 0))
    full = lambda a: pl.BlockSpec(a.shape, lambda i: (0, 0))
    return pl.pallas_call(
        _merge_kernel,
        out_shape=jax.ShapeDtypeStruct((n, D_MODEL), F32),
        grid=(n // tm,),
        in_specs=[row(D_MODEL), row(A_WIDTH), row(B_WIDTH), row(C_WIDTH),
                  full(w_gates), full(wb_a), full(wb_b), full(wb_c), full(w_out), full(g), full(b)],
        out_specs=row(D_MODEL),
        compiler_params=_params(("parallel",)),
        name="merge",
    )(x, ya, yb, yc, w_gates, wb_a, wb_b, wb_c, w_out, g, b)


def _mem_kernel(x_ref, mk_ref, mv_ref, wq_ref, wo_ref, g_ref, b_ref, o_ref, mkb_ref, mvb_ref):
    i = pl.program_id(1)

    @pl.when(i == 0)
    def _():
        mkb_ref[...] = mk_ref[0].astype(BF16)
        mvb_ref[...] = mv_ref[0].astype(BF16)

    x = x_ref[0]
    q = _dot(x.astype(BF16), wq_ref[...]).astype(BF16)
    scale = MEM_HEAD_DIM ** -0.5
    heads = []
    for h in range(MEM_HEADS):
        hs = slice(h * MEM_HEAD_DIM, (h + 1) * MEM_HEAD_DIM)
        s = _dot_nt(q[:, hs], mkb_ref[:, hs]) * scale
        m = s.max(axis=-1, keepdims=True)
        p = jnp.exp(s - m)
        den = p.sum(axis=-1, keepdims=True)
        heads.append((_dot(p.astype(BF16), mvb_ref[:, hs]) / den).astype(BF16))
    o = _dot(jnp.concatenate(heads, axis=1), wo_ref[...])
    o_ref[0] = _layer_norm(ALPHA * x + o, g_ref[...], b_ref[...])


def _mem_attn(x, mem_k, mem_v, first, w_q, w_o, g, b):
    bsz, t, _ = x.shape
    tq = _row_tile(t, 512)
    xspec = pl.BlockSpec((1, tq, D_MODEL), lambda bi, i: (bi, i, 0))
    mspec = pl.BlockSpec((1, N_MEM, D_MODEL), lambda bi, i: (first + bi, 0, 0))
    full = lambda a: pl.BlockSpec(a.shape, lambda bi, i: (0, 0))
    return pl.pallas_call(
        _mem_kernel,
        out_shape=jax.ShapeDtypeStruct((bsz, t, D_MODEL), F32),
        grid=(bsz, t // tq),
        in_specs=[xspec, mspec, mspec, full(w_q), full(w_o), full(g), full(b)],
        out_specs=xspec,
        scratch_shapes=[pltpu.VMEM((N_MEM, D_MODEL), BF16)] * 2,
        compiler_params=_params(("parallel", "arbitrary")),
        name="mem_attn",
    )(x, mem_k, mem_v, w_q, w_o, g, b)


def _memproj_kernel(x_ref, wk_ref, wv_ref, k_ref, v_ref):
    xb = x_ref[...].astype(BF16)
    k_ref[...] = _dot(xb, wk_ref[...])
    v_ref[...] = _dot(xb, wv_ref[...])


def _memproj(mem, w_k, w_v):
    n = mem.shape[0]
    tm = _row_tile(n, 512)
    row = pl.BlockSpec((tm, D_MODEL), lambda i: (i, 0))
    full = pl.BlockSpec((D_MODEL, D_MODEL), lambda i: (0, 0))
    return pl.pallas_call(
        _memproj_kernel,
        out_shape=(jax.ShapeDtypeStruct((n, D_MODEL), F32),) * 2,
        grid=(n // tm,),
        in_specs=[row, full, full],
        out_specs=(row, row),
        compiler_params=_params(("parallel",)),
        name="memproj",
    )(mem, w_k, w_v)


def _pad_rows(a, rows):
    pad = rows - a.shape[1]
    return a if pad == 0 else jnp.pad(a, ((0, 0), (0, pad), (0, 0)))


def _rest_of_layer(x2, ya, yb, yc, mem_k, mem_v, mem_first, b, t, w):
    n = b * t
    x3 = _merge(x2, ya.reshape(n, -1), yb.reshape(n, -1), yc.reshape(n, -1),
                w["gates"], w["wb_a"], w["wb_b"], w["wb_c"], w["w_out"], w["ln_g"][1:2], w["ln_b"][1:2])
    x4 = _mem_attn(x3.reshape(b, t, D_MODEL), mem_k, mem_v, mem_first, w["wm_q"], w["wm_o"],
                   w["ln_g"][2:3], w["ln_b"][2:3])
    x5 = _ffn(x4.reshape(n, D_MODEL), w["f2_gu"], w["f2_d"], w["ln_g"][3:4], w["ln_b"][3:4])
    return x5.reshape(b, t, D_MODEL)


def _prompt_layer(x, mem_k, mem_v, layer, w):
    b, s, _ = x.shape
    n = b * s
    x2 = _ffn(x.reshape(n, D_MODEL), w["f1_gu"], w["f1_d"], w["ln_g"][0:1], w["ln_b"][0:1])
    tab = _rope_tables(jnp.arange(s, dtype=jnp.int32))
    (aq, akt, avt, aktb, avtb, bqt, bkt, bvt, ikt, bkb, bvtb, ikb,
     iqt, iwt, cq, ck4, cv4, cktb, cvb) = _inproj_prompt(x2, w["in_pack"], tab, b, s)
    seq = lambda a: a.reshape(b, s, a.shape[-1])
    ya = _band_prompt(seq(aq), aktb, avtb, w["rel_bias"])
    yb = _dsa_prompt(iqt, iwt, bqt, seq(ikb), seq(bkb), bvtb)
    yc = _diff_prompt(seq(cq), cktb, seq(cvb), w["lam"], w["subln_g"], layer)
    y = _rest_of_layer(x2, ya, yb, yc, mem_k, mem_v, 0, b, s, w)
    keep = min(A_WINDOW, s)
    heads_last = lambda a: jnp.transpose(a[:, :, s - keep:].reshape(b, A_HEADS, HEAD_DIM, keep), (0, 3, 1, 2))
    rows_last = lambda a: jnp.swapaxes(a, 1, 2)
    state = (heads_last(akt), heads_last(avt), rows_last(bkt), rows_last(bvt), rows_last(ikt),
             ck4.reshape(b, s, C_HEADS, 2 * HEAD_DIM), cv4.reshape(b, s, C_HEADS, 2 * HEAD_DIM))
    return y, state


def _step_layer(x, p_len, past, mem_k, mem_v, layer, w):
    b, t, _ = x.shape
    first = layer * b
    n = b * t
    x2 = _ffn(x.reshape(n, D_MODEL), w["f1_gu"], w["f1_d"], w["ln_g"][0:1], w["ln_b"][0:1])
    tm = _row_tile(n, INPROJ_TM)
    assert tm % t == 0
    tab = jnp.tile(_rope_tables(p_len + jnp.arange(t, dtype=jnp.int32)), (1, 1, tm // t, 1))
    outs = _inproj_step(x2, w["in_pack"], tab)
    aq, ak, av, bq, bk, bv, iq, ik, iw, cq, ck, cv = [o.reshape(b, t, o.shape[-1]) for o in outs]
    pa_k, pa_v, pb_k, pb_v, pb_i, pc_k, pc_v = past
    ya = _band_step(aq, ak, av, pa_k, pa_v, w["rel_bias"], first)
    t_pad = -(-t // LANES) * LANES
    pad = lambda a: _pad_rows(a, t_pad)
    yb = _dsa_step(iq, iw, bq, pb_i, pb_k, pb_v, pad(ik), pad(bk), pad(bv), t, first)
    yc = _diff_step(cq, pc_k, pc_v, pad(ck), pad(cv), w["lam"], w["subln_g"], layer, t, first)
    y = _rest_of_layer(x2, ya, yb, yc, mem_k, mem_v, first, b, t, w)
    state = (ak.reshape(b, t, A_HEADS, HEAD_DIM), av.reshape(b, t, A_HEADS, HEAD_DIM), bk, bv, ik,
             ck.reshape(b, t, C_HEADS, 2 * HEAD_DIM), cv.reshape(b, t, C_HEADS, 2 * HEAD_DIM))
    return y, state


def _layer_weights(l, ln_g, ln_b, ffn1_w_gu, ffn1_w_d, ffn2_w_gu, ffn2_w_d, w_in, a_rel_bias, c_lambda,
                   c_subln_g, w_branch_a, w_branch_b, w_branch_c, w_out, w_mem_q, w_mem_k, w_mem_v, w_mem_o):
    wi = w_in[l]
    iw_end = IN_OFFS[9]
    gates_at = IN_OFFS[12]
    in_pack = jnp.concatenate(
        [wi[:, :iw_end], jnp.zeros((D_MODEL, IW_PAD), wi.dtype), wi[:, iw_end:gates_at]], axis=1)
    bf = lambda a: a.astype(BF16)
    return dict(
        ln_g=ln_g[l].astype(F32), ln_b=ln_b[l].astype(F32),
        f1_gu=bf(ffn1_w_gu[l]), f1_d=bf(ffn1_w_d[l]), f2_gu=bf(ffn2_w_gu[l]), f2_d=bf(ffn2_w_d[l]),
        in_pack=bf(in_pack), gates=bf(wi[:, gates_at:]),
        rel_bias=a_rel_bias[l], lam=c_lambda[l].astype(F32), subln_g=c_subln_g[l].astype(F32)[None, :],
        wb_a=bf(w_branch_a[l]), wb_b=bf(w_branch_b[l]), wb_c=bf(w_branch_c[l]), w_out=bf(w_out[l]),
        wm_q=bf(w_mem_q[l]), wm_k=bf(w_mem_k[l]), wm_v=bf(w_mem_v[l]), wm_o=bf(w_mem_o[l]),
    )


def kernel(x_prompt, x_sample, cache_a_k, cache_a_v, cache_b_k, cache_b_v, cache_b_idx, cache_c_k, cache_c_v, cache_mem_k, cache_mem_v, mem_prompt, ln_g, ln_b, ffn1_w_gu, ffn1_w_d, ffn2_w_gu, ffn2_w_d, w_in, a_rel_bias, c_lambda, c_subln_g, w_branch_a, w_branch_b, w_branch_c, w_out, w_mem_q, w_mem_k, w_mem_v, w_mem_o):
    bp, s, _ = x_prompt.shape
    bs, t, _ = x_sample.shape
    p_len = cache_b_k.shape[2]
    depth = ln_g.shape[0]
    yp, ys = x_prompt, x_sample
    st_p, st_s, mk_p, mv_p = [], [], [], []
    rows = depth * bs
    keys_last = lambda a: jnp.transpose(a, (0, 1, 3, 2)).reshape(rows, a.shape[3], a.shape[2])
    pairs = lambda a: a.reshape(rows, a.shape[2] * C_HEADS, 2 * HEAD_DIM)
    past = (cache_a_k.reshape(rows, -1, A_WIDTH), cache_a_v.reshape(rows, -1, A_WIDTH),
            keys_last(cache_b_k), keys_last(cache_b_v), keys_last(cache_b_idx), pairs(cache_c_k), pairs(cache_c_v))
    mem_k_s = cache_mem_k.reshape(rows, N_MEM, D_MODEL)
    mem_v_s = cache_mem_v.reshape(rows, N_MEM, D_MODEL)
    for l in range(depth):
        w = _layer_weights(l, ln_g, ln_b, ffn1_w_gu, ffn1_w_d, ffn2_w_gu, ffn2_w_d, w_in, a_rel_bias,
                           c_lambda, c_subln_g, w_branch_a, w_branch_b, w_branch_c, w_out,
                           w_mem_q, w_mem_k, w_mem_v, w_mem_o)
        mk, mv = _memproj(mem_prompt.reshape(bp * N_MEM, D_MODEL), w["wm_k"], w["wm_v"])
        mk = mk.reshape(bp, N_MEM, D_MODEL)
        mv = mv.reshape(bp, N_MEM, D_MODEL)
        yp, sp = _prompt_layer(yp, mk, mv, l, w)
        ys, ss = _step_layer(ys, p_len, past, mem_k_s, mem_v_s, l, w)
        st_p.append(sp)
        st_s.append(ss)
        mk_p.append(mk.reshape(bp, N_MEM, MEM_HEADS, MEM_HEAD_DIM))
        mv_p.append(mv.reshape(bp, N_MEM, MEM_HEADS, MEM_HEAD_DIM))
    stk = lambda sts: [jnp.stack([st[i] for st in sts]) for i in range(7)]
    (a_k_p, a_v_p, b_k_p, b_v_p, b_i_p, c_k_p, c_v_p) = stk(st_p)
    (a_k_s, a_v_s, b_k_s, b_v_s, b_i_s, c_k_s, c_v_s) = stk(st_s)
    return (yp, ys, a_k_p, a_v_p, b_k_p, b_v_p, b_i_p, c_k_p, c_v_p,
            jnp.stack(mk_p), jnp.stack(mv_p), a_k_s, a_v_s, b_k_s, b_v_s, b_i_s, c_k_s, c_v_s)
```

```python
import functools
import math

import jax
import jax.numpy as jnp
import numpy as np
from jax import lax
from jax.experimental import pallas as pl
from jax.experimental.pallas import tpu as pltpu

F32 = jnp.float32
BF16 = jnp.bfloat16

D_MODEL = 1024
CHUNK = 64
HEAD_DIM = 64
ROT_DIM = HEAD_DIM // 4
ROT_HALF = ROT_DIM // 2
ROPE_THETA = 500000.0
A_HEADS = 4
A_LEFT_CHUNKS = 8
A_WINDOW = A_LEFT_CHUNKS * CHUNK
REL_CLIP = 128
B_HEADS = 4
IDX_HEADS = 8
IDX_DIM = 64
TOPK_MAX = 256
C_HEADS = 4
N_MEM = 256
MEM_HEADS = 4
MEM_HEAD_DIM = D_MODEL // MEM_HEADS
D_FF = ((8 * D_MODEL // 3 + 127) // 128) * 128
N_BRANCH = 3
DEPTH = 2
ALPHA = (2.0 * DEPTH) ** 0.25
LN_EPS = 1e-5
SUBLN_EPS = 1e-5
A_WIDTH = A_HEADS * HEAD_DIM
B_WIDTH = B_HEADS * HEAD_DIM
C_WIDTH = C_HEADS * 2 * HEAD_DIM
IN_SIZES = (A_WIDTH, A_WIDTH, A_WIDTH,
            B_WIDTH, HEAD_DIM, HEAD_DIM, IDX_HEADS * IDX_DIM, IDX_DIM, IDX_HEADS,
            C_WIDTH, C_WIDTH, C_WIDTH,
            N_BRANCH * D_MODEL)
IN_OFFS = tuple(int(v) for v in np.cumsum((0,) + IN_SIZES))

LANES = 128
V7X_VMEM_LIMIT = 56 * 1024 * 1024

NEG = -1e30
LOG2E = math.log2(math.e)
INT_MIN = -2 ** 31

IW_PAD = LANES - HEAD_DIM - IDX_HEADS
P_AQ, P_AK, P_AV = 0, 256, 512
P_B = 768
P_I = 1152
P_C = 1792
P_END = 3328


def _dot(a, b):
    return jnp.dot(a, b, preferred_element_type=F32)


def _dot_nt(a, b):
    return lax.dot_general(a, b, (((1,), (1,)), ((), ())), preferred_element_type=F32)


def _layer_norm(v, g, b):
    mu = jnp.mean(v, axis=-1, keepdims=True)
    d = v - mu
    var = jnp.mean(d * d, axis=-1, keepdims=True)
    return d * lax.rsqrt(var + LN_EPS) * g + b


def _params(sem):
    return pltpu.CompilerParams(dimension_semantics=sem, vmem_limit_bytes=V7X_VMEM_LIMIT)


def _row_tile(n, want):
    t = min(n, want)
    assert n % t == 0
    return t


def _call_into(body, n_in, prev, **kw):
    if prev is None:
        return pl.pallas_call(body, **kw)
    n_prev = len(prev)
    kernel = lambda *refs: body(*refs[:n_in], *refs[n_in + n_prev:])
    kw["in_specs"] = list(kw["in_specs"]) + [pl.BlockSpec(memory_space=pl.ANY)] * n_prev
    call = pl.pallas_call(kernel, input_output_aliases={n_in + i: i for i in range(n_prev)}, **kw)
    return lambda *args: call(*args, *prev)


def _ffn_kernel(x_ref, wg_ref, wu_ref, wd_ref, g_ref, b_ref, o_ref, xb_ref, acc_ref):
    j = pl.program_id(1)

    @pl.when(j == 0)
    def _():
        xb_ref[...] = x_ref[...].astype(BF16)
        acc_ref[...] = jnp.zeros_like(acc_ref)

    xb = xb_ref[...]
    g = _dot(xb, wg_ref[...])
    u = _dot(xb, wu_ref[...])
    h = g * jax.nn.sigmoid(g) * u
    acc_ref[...] += _dot(h.astype(BF16), wd_ref[...])

    @pl.when(j == pl.num_programs(1) - 1)
    def _():
        o_ref[...] = _layer_norm(ALPHA * x_ref[...] + 0.5 * acc_ref[...], g_ref[...], b_ref[...])


def _ffn(x, w_gu, w_d, g, b):
    n = x.shape[0]
    tm = _row_tile(n, 1024)
    tf = 256
    nf = D_FF // tf
    return pl.pallas_call(
        _ffn_kernel,
        out_shape=jax.ShapeDtypeStruct((n, D_MODEL), F32),
        grid=(n // tm, nf),
        in_specs=[
            pl.BlockSpec((tm, D_MODEL), lambda i, j: (i, 0)),
            pl.BlockSpec((D_MODEL, tf), lambda i, j: (0, j)),
            pl.BlockSpec((D_MODEL, tf), lambda i, j: (0, j + nf)),
            pl.BlockSpec((tf, D_MODEL), lambda i, j: (j, 0)),
            pl.BlockSpec((1, D_MODEL), lambda i, j: (0, 0)),
            pl.BlockSpec((1, D_MODEL), lambda i, j: (0, 0)),
        ],
        out_specs=pl.BlockSpec((tm, D_MODEL), lambda i, j: (i, 0)),
        scratch_shapes=[pltpu.VMEM((tm, D_MODEL), BF16), pltpu.VMEM((tm, D_MODEL), F32)],
        compiler_params=_params(("parallel", "arbitrary")),
        name="ffn",
    )(x, w_gu, w_gu, w_d, g, b)


def _rope_tables(pos):
    t = pos.shape[0]
    inv_freq = ROPE_THETA ** (-jnp.arange(ROT_HALF, dtype=F32) / ROT_HALF)
    ang = pos.astype(F32)[:, None] * inv_freq
    cos, sin = jnp.cos(ang), jnp.sin(ang)
    rest = HEAD_DIM - ROT_DIM
    ones = lambda w: jnp.ones((t, w), F32)
    zeros = lambda w: jnp.zeros((t, w), F32)
    c64 = jnp.concatenate([cos, cos, ones(rest)], axis=1)
    lo64 = jnp.concatenate([-sin, zeros(ROT_HALF), zeros(rest)], axis=1)
    hi64 = jnp.concatenate([zeros(ROT_HALF), sin, zeros(rest)], axis=1)
    iw64 = jnp.concatenate([jnp.full((t, IDX_HEADS), IDX_HEADS ** -0.5, F32), ones(HEAD_DIM - IDX_HEADS)], axis=1)
    cat = lambda a, b_: jnp.concatenate([a, b_], axis=1)
    both = jnp.stack([cat(c64, c64), cat(lo64, lo64), cat(hi64, hi64)])
    upper_id = jnp.stack([cat(c64, ones(HEAD_DIM)), cat(lo64, zeros(HEAD_DIM)), cat(hi64, zeros(HEAD_DIM))])
    upper_iw = jnp.stack([cat(c64, iw64), cat(lo64, zeros(HEAD_DIM)), cat(hi64, zeros(HEAD_DIM))])
    return jnp.stack([both, upper_id, upper_iw])


def _rope_apply(z, tab_ref, variant):
    cos = tab_ref[variant, 0]
    lo = tab_ref[variant, 1]
    hi = tab_ref[variant, 2]
    outs = []
    for k in range(z.shape[1] // LANES):
        zc = z[:, k * LANES:(k + 1) * LANES]
        outs.append(zc * cos + pltpu.roll(zc, LANES - ROT_HALF, 1) * lo + pltpu.roll(zc, ROT_HALF, 1) * hi)
    return outs[0] if len(outs) == 1 else jnp.concatenate(outs, axis=1)


def _inproj_sections(x_ref, w_ref, tab_ref):
    xb = x_ref[...].astype(BF16)
    za = _dot(xb, w_ref[:, P_AQ:P_B])
    zb = _dot(xb, w_ref[:, P_B:P_I])
    zi = _dot(xb, w_ref[:, P_I:P_C])
    zc = _dot(xb, w_ref[:, P_C:P_END])
    qs = HEAD_DIM ** -0.5 * LOG2E
    return dict(
        aq=za[:, 0:256] * qs, ak=za[:, 256:512], av=za[:, 512:768],
        bq=_rope_apply(zb[:, 0:256], tab_ref, 0) * qs,
        bkv=_rope_apply(zb[:, 256:384], tab_ref, 1),
        iq=_rope_apply(zi[:, 0:512], tab_ref, 0),
        ikw=_rope_apply(zi[:, 512:640], tab_ref, 2),
        cq=_rope_apply(zc[:, 0:512], tab_ref, 0) * qs,
        ck=_rope_apply(zc[:, 512:1024], tab_ref, 0),
        cv=zc[:, 1024:1536])


def _store_heads_interleaved(o_ref, v, heads):
    rows = v.shape[0]
    for h in range(heads):
        o_ref[pl.ds(h, rows, stride=heads), :] = v[:, h * LANES:(h + 1) * LANES]


def _inproj_prompt_kernel(x_ref, w_ref, tab_ref, ck4_ref, cv4_ref,
                          aq_ref, akt_ref, avt_ref, aktb_ref, avtb_ref,
                          bqt_ref, bkt_ref, bvt_ref, ikt_ref, bkb_ref, bvtb_ref, ikb_ref,
                          iqt_ref, iwt_ref, cq_ref, cktb_ref, cvb_ref):
    s = _inproj_sections(x_ref, w_ref, tab_ref)
    aq_ref[...] = s["aq"].astype(BF16)
    akt = s["ak"].T
    avt = s["av"].T
    akt_ref[0] = akt
    avt_ref[0] = avt
    aktb_ref[0] = akt.astype(BF16)
    avtb_ref[0] = avt.astype(BF16)
    bqt_ref[0] = s["bq"].T.astype(BF16)
    kvt = s["bkv"].T
    bkt_ref[0] = kvt[0:HEAD_DIM]
    bvt_ref[0] = kvt[HEAD_DIM:2 * HEAD_DIM]
    bkb_ref[...] = s["bkv"][:, 0:HEAD_DIM].astype(BF16)
    bvtb_ref[0] = kvt[HEAD_DIM:2 * HEAD_DIM].astype(BF16)
    kwt = s["ikw"].T
    ikt_ref[0] = kwt[0:IDX_DIM]
    ikb_ref[...] = s["ikw"][:, 0:IDX_DIM].astype(BF16)
    iqt_ref[0] = s["iq"].T.astype(BF16)
    iwt_ref[0] = kwt[IDX_DIM:IDX_DIM + IDX_HEADS]
    cq_ref[...] = s["cq"].astype(BF16)
    _store_heads_interleaved(ck4_ref, s["ck"], C_HEADS)
    _store_heads_interleaved(cv4_ref, s["cv"], C_HEADS)
    cktb_ref[0] = s["ck"].T.astype(BF16)
    cvb_ref[...] = s["cv"].astype(BF16)


INPROJ_TM = 256


def _inproj_prompt(x, w_pack, tab, bsz, seq, layer, depth, prev):
    n = x.shape[0]
    tm = INPROJ_TM
    spb = seq // tm
    nblk = n // tm
    tok = lambda w: pl.BlockSpec((tm, w), lambda i: (i, 0))
    tr = lambda d: pl.BlockSpec((1, d, tm), lambda i: (i // spb, 0, i % spb))
    sds = jax.ShapeDtypeStruct
    tok_o = lambda w, dt: (sds((n, w), dt), tok(w))
    tr_o = lambda d, dt: (sds((bsz, d, seq), dt), tr(d))
    il_o = (sds((depth * n * C_HEADS, LANES), F32),
            pl.BlockSpec((tm * C_HEADS, LANES), lambda i: (layer * nblk + i, 0)))
    outs = [il_o, il_o,
            tok_o(A_WIDTH, BF16), tr_o(A_WIDTH, F32), tr_o(A_WIDTH, F32), tr_o(A_WIDTH, BF16), tr_o(A_WIDTH, BF16),
            tr_o(B_WIDTH, BF16), tr_o(HEAD_DIM, F32), tr_o(HEAD_DIM, F32), tr_o(IDX_DIM, F32),
            tok_o(HEAD_DIM, BF16), tr_o(HEAD_DIM, BF16), tok_o(IDX_DIM, BF16),
            tr_o(IDX_HEADS * IDX_DIM, BF16), tr_o(IDX_HEADS, F32), tok_o(C_WIDTH, BF16),
            tr_o(C_WIDTH, BF16), tok_o(C_WIDTH, BF16)]
    return _call_into(
        _inproj_prompt_kernel, 3, prev,
        out_shape=tuple(o[0] for o in outs),
        grid=(n // tm,),
        in_specs=[tok(D_MODEL),
                  pl.BlockSpec((D_MODEL, P_END), lambda i: (0, 0)),
                  pl.BlockSpec((3, 3, tm, LANES), lambda i: (0, 0, i % spb, 0))],
        out_specs=tuple(o[1] for o in outs),
        compiler_params=_params(("parallel",)),
        name="inproj_prompt",
    )(x, w_pack, tab)


def _inproj_step_kernel(x_ref, w_ref, tab_ref,
                        aq_ref, ak_ref, av_ref, bq_ref, bk_ref, bv_ref,
                        iq_ref, ik_ref, iw_ref, cq_ref, ck_ref, cv_ref):
    s = _inproj_sections(x_ref, w_ref, tab_ref)
    aq_ref[...] = s["aq"].astype(BF16)
    ak_ref[...] = s["ak"]
    av_ref[...] = s["av"]
    bq_ref[...] = s["bq"].astype(BF16)
    bk_ref[...] = s["bkv"][:, 0:HEAD_DIM]
    bv_ref[...] = s["bkv"][:, HEAD_DIM:2 * HEAD_DIM]
    iq_ref[...] = s["iq"].astype(BF16)
    ik_ref[...] = s["ikw"][:, 0:IDX_DIM]
    iw_ref[...] = s["ikw"][:, IDX_DIM:IDX_DIM + IDX_HEADS]
    cq_ref[...] = s["cq"].astype(BF16)
    ck_ref[...] = s["ck"]
    cv_ref[...] = s["cv"]


def _inproj_step(x, w_pack, tab):
    n = x.shape[0]
    tm = _row_tile(n, INPROJ_TM)
    widths = ((A_WIDTH, BF16), (A_WIDTH, F32), (A_WIDTH, F32), (B_WIDTH, BF16), (HEAD_DIM, F32), (HEAD_DIM, F32),
              (IDX_HEADS * IDX_DIM, BF16), (IDX_DIM, F32), (IDX_HEADS, F32),
              (C_WIDTH, BF16), (C_WIDTH, F32), (C_WIDTH, F32))
    return pl.pallas_call(
        _inproj_step_kernel,
        out_shape=tuple(jax.ShapeDtypeStruct((n, w), dt) for w, dt in widths),
        grid=(n // tm,),
        in_specs=[
            pl.BlockSpec((tm, D_MODEL), lambda i: (i, 0)),
            pl.BlockSpec((D_MODEL, P_END), lambda i: (0, 0)),
            pl.BlockSpec((3, 3, tm, LANES), lambda i: (0, 0, 0, 0)),
        ],
        out_specs=tuple(pl.BlockSpec((tm, w), lambda i: (i, 0)) for w, _ in widths),
        compiler_params=_params(("parallel",)),
        name="inproj_step",
    )(x, w_pack, tab)


def _qk(q, piece):
    kind, k = piece
    return _dot(q, k) if kind == "T" else _dot_nt(q, k)


def _pv(p, piece):
    kind, v = piece
    return _dot_nt(p, v) if kind == "T" else _dot(p, v)


def _piece_len(piece):
    kind, a = piece
    return a.shape[1] if kind == "T" else a.shape[0]


def _cat(parts):
    return parts[0] if len(parts) == 1 else jnp.concatenate(parts, axis=1)


def _pv_pieces(p, pieces):
    off = 0
    o = None
    for piece in pieces:
        n = _piece_len(piece)
        t = _pv(p[:, off:off + n], piece)
        o = t if o is None else o + t
        off += n
    return o


def _toeplitz_bias(rel_bias, q_rows, k_cols, q_off):
    period = pl.next_power_of_2(q_rows + k_cols)
    e = np.arange(period)
    e = np.where(e >= k_cols, e - period, e)
    idx = np.clip(q_off - e, -REL_CLIP, REL_CLIP) + REL_CLIP
    table = rel_bias[:, idx].astype(F32) * LOG2E
    flat = jnp.tile(table, (1, q_rows + 1))[:, :q_rows * (period - 1)]
    return flat.reshape(rel_bias.shape[0], q_rows, period - 1)[:, :, :k_cols]


def _band_attend(q, pieces, bias_ref, o_ref):
    for h in range(A_HEADS):
        hs = slice(h * HEAD_DIM, (h + 1) * HEAD_DIM)
        qh = q[:, hs]
        ss = []
        off = 0
        for k_of, _, extra in pieces:
            kp = k_of(h)
            n = _piece_len(kp)
            s = _qk(qh, kp) + bias_ref[h, :, off:off + n]
            if extra is not None:
                s = s + extra
            ss.append(s)
            off += n
        m = ss[0].max(axis=-1, keepdims=True)
        for s in ss[1:]:
            m = jnp.maximum(m, s.max(axis=-1, keepdims=True))
        l = jnp.zeros_like(m)
        o = jnp.zeros((q.shape[0], HEAD_DIM), F32)
        for s, (_, v_of, _) in zip(ss, pieces):
            p = jnp.exp2(s - m)
            l = l + p.sum(axis=-1, keepdims=True)
            o = o + _pv(p.astype(BF16), v_of(h))
        o_ref[0, :, hs] = (o / l).astype(o_ref.dtype)


BAND_QB = 256
BAND_KB = 3


def _band_prompt_kernel(q_ref, k0_ref, k1_ref, k2_ref, v0_ref, v1_ref, v2_ref, bias_ref, o_ref):
    j = pl.program_id(1)
    pieces = []
    for kb, (k_ref, v_ref) in enumerate(((k0_ref, v0_ref), (k1_ref, v1_ref), (k2_ref, v2_ref))):
        back = BAND_KB - 1 - kb
        extra = None if back == 0 else jnp.where(j >= back, 0.0, NEG)
        head_rows = lambda ref: (lambda h: ("T", ref[0, h * HEAD_DIM:(h + 1) * HEAD_DIM, :]))
        pieces.append((head_rows(k_ref), head_rows(v_ref), extra))
    _band_attend(q_ref[0], pieces, bias_ref, o_ref)


def _band_prompt(aq, akt, avt, rel_bias):
    b, s, _ = aq.shape
    assert A_WINDOW == (BAND_KB - 1) * BAND_QB and s % BAND_QB == 0
    kw = BAND_KB * BAND_QB
    r = np.arange(BAND_QB)[:, None] + A_WINDOW
    c = np.arange(kw)[None, :]
    in_band = (c // CHUNK <= r // CHUNK) & (c // CHUNK >= r // CHUNK - A_LEFT_CHUNKS)
    bias = jnp.where(in_band[None], _toeplitz_bias(rel_bias, BAND_QB, kw, A_WINDOW), NEG)
    qspec = pl.BlockSpec((1, BAND_QB, A_WIDTH), lambda bi, j: (bi, j, 0))
    kspecs = [pl.BlockSpec((1, A_WIDTH, BAND_QB),
                           functools.partial(lambda bi, j, back: (bi, 0, jnp.maximum(j - back, 0)),
                                             back=BAND_KB - 1 - kb))
              for kb in range(BAND_KB)]
    return pl.pallas_call(
        _band_prompt_kernel,
        out_shape=jax.ShapeDtypeStruct((b, s, A_WIDTH), BF16),
        grid=(b, s // BAND_QB),
        in_specs=[qspec] + kspecs + kspecs + [pl.BlockSpec((A_HEADS, BAND_QB, kw), lambda bi, j: (0, 0, 0))],
        out_specs=qspec,
        compiler_params=_params(("parallel", "parallel")),
        name="band_prompt",
    )(aq, akt, akt, akt, avt, avt, avt, bias)


def _band_step_kernel(q_ref, pk_ref, k_ref, pv_ref, v_ref, bias_ref, o_ref):
    head_cols = lambda a: (lambda h: ("N", a[:, h * HEAD_DIM:(h + 1) * HEAD_DIM]))
    pieces = [(head_cols(pk_ref[0].astype(BF16)), head_cols(pv_ref[0].astype(BF16)), None),
              (head_cols(k_ref[0].astype(BF16)), head_cols(v_ref[0].astype(BF16)), None)]
    _band_attend(q_ref[0], pieces, bias_ref, o_ref)


def _band_step(aq, ak, av, pk, pv, rel_bias, first):
    b, t, _ = aq.shape
    w = pk.shape[1]
    bias = _toeplitz_bias(rel_bias, t, w + t, w)
    new = pl.BlockSpec((1, t, A_WIDTH), lambda bi: (bi, 0, 0))
    old = pl.BlockSpec((1, w, A_WIDTH), lambda bi: (first + bi, 0, 0))
    return pl.pallas_call(
        _band_step_kernel,
        out_shape=jax.ShapeDtypeStruct((b, t, A_WIDTH), BF16),
        grid=(b,),
        in_specs=[new, old, new, old, new, pl.BlockSpec((A_HEADS, t, w + t), lambda bi: (0, 0, 0))],
        out_specs=new,
        compiler_params=_params(("parallel",)),
        name="band_step",
    )(aq, pk, ak, pv, av, bias)


def _admissible(q0, q_rows, k0, k_cols, k_true):
    kpos = lax.broadcasted_iota(jnp.int32, (q_rows, k_cols), 1) + k0
    qpos = lax.broadcasted_iota(jnp.int32, (q_rows, k_cols), 0) + q0
    shift = CHUNK.bit_length() - 1
    return ((kpos >> shift) <= (qpos >> shift)) & (kpos < k_true)


def _free_cols(q0, k_cols, k_true):
    return min((q0 // CHUNK + 1) * CHUNK, k_true, k_cols) // LANES * LANES


def _count_ge(ref, cand16, width):
    acc = None
    for t in range(width // LANES):
        one = jnp.where(ref[:, t * LANES:(t + 1) * LANES] >= cand16, jnp.int16(1), jnp.int16(0))
        acc = one if acc is None else acc + one
    return jnp.sum(acc.astype(F32), axis=1, keepdims=True)


def _search16(ref, want, width):
    def body(it, prefix):
        cand = prefix | jnp.left_shift(jnp.int32(1), 15 - it)
        cnt = _count_ge(ref, (cand - 32768).astype(jnp.int16), width)
        return jnp.where(cnt >= want, cand, prefix)
    return lax.fori_loop(0, 16, body, jnp.zeros((ref.shape[0], 1), jnp.int32))


def _select_topk(score, topk, hi_ref, lo_ref):
    q_rows, l = score.shape
    bits = lax.bitcast_convert_type(score, jnp.int32)
    key = bits ^ ((bits >> 31) & jnp.int32(0x7FFFFFFF))
    kf = jnp.float32(topk)
    hi_ref[...] = (key >> 16).astype(jnp.int16)
    hi_thr = _search16(hi_ref, kf, l) - 32768
    hi_thr16 = hi_thr.astype(jnp.int16)
    above = _count_ge(hi_ref, hi_thr16 + jnp.int16(1), l) * jnp.where(hi_thr < 32767, 1.0, 0.0)
    low = ((key & jnp.int32(0xFFFF)) - 32768).astype(jnp.int16)
    lo_ref[...] = jnp.where(hi_ref[...] == hi_thr16, low, jnp.int16(-32768))
    lo_thr = _search16(lo_ref, kf - above, l)
    thr = (hi_thr << 16) | lo_thr
    need = kf - jnp.sum(jnp.where(key > thr, 1.0, 0.0), axis=1, keepdims=True)
    rr = lax.broadcasted_iota(jnp.int32, (LANES, LANES), 0)
    cc = lax.broadcasted_iota(jnp.int32, (LANES, LANES), 1)
    before = jnp.where(rr < cc, 1.0, 0.0).astype(BF16)
    carry = jnp.zeros((q_rows, 1), F32)
    sel = []
    for t in range(l // LANES):
        key_t = key[:, t * LANES:(t + 1) * LANES]
        eq_t = jnp.where(key_t == thr, 1.0, 0.0)
        rank = carry + _dot(eq_t.astype(BF16), before)
        take_tie = jnp.where(rank < need, eq_t, 0.0)
        sel.append(jnp.where(key_t > thr, 1.0, take_tie))
        carry = carry + jnp.sum(eq_t, axis=1, keepdims=True)
    return jnp.concatenate(sel, axis=1)


def _dsa_attend(q0, k_true, topk, iq, iw, bq, ik_pieces, bk_pieces, bv_pieces, o_ref, hi_ref, lo_ref):
    qn = iq.shape[0]
    l = sum(_piece_len(p) for p in ik_pieces)
    score = None
    for h in range(IDX_HEADS):
        hs = slice(h * IDX_DIM, (h + 1) * IDX_DIM)
        dots = _cat([_qk(iq[:, hs], p) for p in ik_pieces])
        term = iw[:, h:h + 1] * jnp.maximum(dots, 0.0)
        score = term if score is None else score + term
    free = _free_cols(q0, l, k_true)
    adm = _admissible(q0, qn, free, l - free, k_true)
    if k_true <= topk:
        tail = jnp.where(adm, 0.0, NEG)
        mask = tail if free == 0 else jnp.concatenate([jnp.zeros((qn, free), F32), tail], axis=1)
    else:
        score = score + 0.0
        tail = jnp.where(adm, score[:, free:], -jnp.inf)
        score = tail if free == 0 else jnp.concatenate([score[:, :free], tail], axis=1)
        sel = _select_topk(score, topk, hi_ref, lo_ref)
        tail = jnp.where(adm, jnp.where(sel[:, free:] > 0.0, 0.0, NEG), NEG)
        mask = tail if free == 0 else jnp.concatenate([jnp.where(sel[:, :free] > 0.0, 0.0, NEG), tail], axis=1)
    for h in range(B_HEADS):
        hs = slice(h * HEAD_DIM, (h + 1) * HEAD_DIM)
        s = _cat([_qk(bq[:, hs], p) for p in bk_pieces]) + mask
        m = s.max(axis=-1, keepdims=True)
        p = jnp.exp2(s - m)
        den = p.sum(axis=-1, keepdims=True)
        o = _pv_pieces(p.astype(BF16), bv_pieces)
        o_ref[0, :, hs] = (o / den).astype(o_ref.dtype)


def _bit_planes(key):
    a = [key[8 * j:8 * (j + 1), :] for j in range(32)]
    j, m = 16, 0x0000FFFF
    while j:
        for k in range(32):
            if k & j:
                continue
            t = (lax.shift_right_logical(a[k], jnp.int32(j)) ^ a[k | j]) & jnp.int32(m - (1 << 32) if m >> 31 else m)
            a[k] = a[k] ^ (t << j)
            a[k | j] = a[k | j] ^ t
        j >>= 1
        m = (m ^ (m << j)) & 0xFFFFFFFF
    return a


BITS_GROUP = 256


def _kth_largest_t(key, topk):
    l, qn = key.shape
    groups = l // BITS_GROUP
    ukey = key ^ jnp.int32(INT_MIN)
    planes = [_bit_planes(ukey[g * BITS_GROUP:(g + 1) * BITS_GROUP, :]) for g in range(groups)]
    alive = [jnp.full((8, qn), -1, jnp.int32) for _ in range(groups)]
    above = jnp.zeros((1, qn), F32)
    thr = jnp.zeros((1, qn), jnp.int32)
    kf = jnp.float32(topk)
    for b in range(31, -1, -1):
        with_bit = [alive[g] & planes[g][b] for g in range(groups)]
        cnt = lax.population_count(with_bit[0])
        for g in range(1, groups):
            cnt = cnt + lax.population_count(with_bit[g])
        c = above + jnp.sum(cnt.astype(F32), axis=0, keepdims=True)
        take = c >= kf
        thr = jnp.where(take, thr | jnp.int32((1 << b) - (1 << 32) if b == 31 else (1 << b)), thr)
        above = jnp.where(take, above, c)
        alive = [jnp.where(take, with_bit[g], alive[g] ^ with_bit[g]) for g in range(groups)]
    return thr ^ jnp.int32(INT_MIN), above


def _select_topk_t(score, topk):
    l, qn = score.shape
    bits = lax.bitcast_convert_type(score, jnp.int32)
    key = bits ^ ((bits >> 31) & jnp.int32(0x7FFFFFFF))
    thr, above = _kth_largest_t(key, topk)
    need = jnp.float32(topk) - above
    rr = lax.broadcasted_iota(jnp.int32, (LANES, LANES), 0)
    cc = lax.broadcasted_iota(jnp.int32, (LANES, LANES), 1)
    earlier = jnp.where(cc < rr, 1.0, 0.0).astype(BF16)
    carry = jnp.zeros((1, qn), F32)
    sel = []
    for t in range(l // LANES):
        key_t = key[t * LANES:(t + 1) * LANES, :]
        eq_t = jnp.where(key_t == thr, 1.0, 0.0)
        rank = carry + _dot(earlier, eq_t.astype(BF16))
        take_tie = jnp.where(rank < need, eq_t, 0.0)
        sel.append(jnp.where(key_t > thr, 1.0, take_tie))
        carry = carry + jnp.sum(eq_t, axis=0, keepdims=True)
    return jnp.concatenate(sel, axis=0)


def _dsa_prompt_kernel(q0, k_len, topk, iqt_ref, iwt_ref, bqt_ref, ik_ref, bk_ref, bvt_ref, o_ref):
    qn = iqt_ref.shape[2]
    ik = ik_ref[0]
    score = None
    for h in range(IDX_HEADS):
        dots = _dot(ik, iqt_ref[0, h * IDX_DIM:(h + 1) * IDX_DIM, :])
        term = iwt_ref[0, h:h + 1, :] * jnp.maximum(dots, 0.0)
        score = term if score is None else score + term
    free = _free_cols(q0, k_len, k_len)
    kpos = lax.broadcasted_iota(jnp.int32, (k_len - free, qn), 0) + free
    qpos = lax.broadcasted_iota(jnp.int32, (k_len - free, qn), 1) + q0
    shift = CHUNK.bit_length() - 1
    adm = (kpos >> shift) <= (qpos >> shift)
    if k_len <= topk:
        tail = jnp.where(adm, 0.0, NEG)
        mask = tail if free == 0 else jnp.concatenate([jnp.zeros((free, qn), F32), tail], axis=0)
    else:
        score = score + 0.0
        tail = jnp.where(adm, score[free:, :], -jnp.inf)
        score = tail if free == 0 else jnp.concatenate([score[:free, :], tail], axis=0)
        sel = _select_topk_t(score, topk)
        tail = jnp.where(adm, jnp.where(sel[free:, :] > 0.0, 0.0, NEG), NEG)
        mask = tail if free == 0 else jnp.concatenate([jnp.where(sel[:free, :] > 0.0, 0.0, NEG), tail], axis=0)
    bk = bk_ref[0]
    outs = []
    for h in range(B_HEADS):
        s = _dot(bk, bqt_ref[0, h * HEAD_DIM:(h + 1) * HEAD_DIM, :]) + mask
        m = s.max(axis=0, keepdims=True)
        p = jnp.exp2(s - m)
        den = p.sum(axis=0, keepdims=True)
        outs.append(_dot(bvt_ref[0], p.astype(BF16)) / den)
    o_ref[0] = jnp.concatenate(outs, axis=0).T.astype(o_ref.dtype)


CAUSAL_QB = 256


def _dsa_prompt(iqt, iwt, bqt, ikb, bkb, bvtb):
    b, _, s = iqt.shape
    qb = CAUSAL_QB
    topk = min(TOPK_MAX, s // 4)
    out = None
    for k in range(s // qb):
        k_len = (k + 1) * qb
        qspec = lambda d, k=k: pl.BlockSpec((1, d, qb), lambda bi: (bi, 0, k))
        rows = pl.BlockSpec((1, k_len, HEAD_DIM), lambda bi: (bi, 0, 0))
        out = _call_into(
            functools.partial(_dsa_prompt_kernel, k * qb, k_len, topk), 6, None if out is None else (out,),
            out_shape=jax.ShapeDtypeStruct((b, s, B_WIDTH), BF16),
            grid=(b,),
            in_specs=[qspec(IDX_HEADS * IDX_DIM), qspec(IDX_HEADS), qspec(B_WIDTH), rows, rows,
                      pl.BlockSpec((1, HEAD_DIM, k_len), lambda bi: (bi, 0, 0))],
            out_specs=pl.BlockSpec((1, qb, B_WIDTH), functools.partial(lambda bi, k: (bi, k, 0), k=k)),
            compiler_params=_params(("parallel",)),
            name="dsa_prompt",
        )(iqt, iwt, bqt, ikb, bkb, bvtb)
    return out


def _dsa_step_kernel(q0, k_true, topk, iq_ref, iw_ref, bq_ref, pik_ref, pbk_ref, pbv_ref,
                     ik_ref, bk_ref, bv_ref, o_ref, hi_ref, lo_ref):
    two = lambda old, new: [("T", old[0].astype(BF16)), ("N", new[0].astype(BF16))]
    _dsa_attend(q0, k_true, topk, iq_ref[0], iw_ref[0], bq_ref[0],
                two(pik_ref, ik_ref), two(pbk_ref, bk_ref), two(pbv_ref, bv_ref), o_ref, hi_ref, lo_ref)


def _dsa_step(iq, iw, bq, pik, pbk, pbv, ik, bk, bv, t_true, first):
    b, t, _ = iq.shape
    p_len = pik.shape[2]
    assert p_len % LANES == 0 and ik.shape[1] % LANES == 0
    k_true = p_len + t_true
    topk = min(TOPK_MAX, k_true // 4)
    qspec = lambda w: pl.BlockSpec((1, t, w), lambda bi: (bi, 0, 0))
    old = pl.BlockSpec((1, HEAD_DIM, p_len), lambda bi: (first + bi, 0, 0))
    new = pl.BlockSpec((1, ik.shape[1], HEAD_DIM), lambda bi: (bi, 0, 0))
    return pl.pallas_call(
        functools.partial(_dsa_step_kernel, p_len, k_true, topk),
        out_shape=jax.ShapeDtypeStruct((b, t, B_WIDTH), BF16),
        grid=(b,),
        in_specs=[qspec(IDX_HEADS * IDX_DIM), qspec(IDX_HEADS), qspec(B_WIDTH), old, old, old, new, new, new],
        out_specs=qspec(B_WIDTH),
        scratch_shapes=[pltpu.VMEM((t, p_len + ik.shape[1]), jnp.int16)] * 2,
        compiler_params=_params(("parallel",)),
        name="dsa_step",
    )(iq, iw, bq, pik, pbk, pbv, ik, bk, bv)


def _diff_attend(q0, k_true, lam_init, q, k_of, v_of, lam_ref, g_ref, o_ref):
    qn = q.shape[0]
    lp = lam_ref[...]
    lam = (jnp.exp(jnp.sum(lp[0:1] * lp[1:2], axis=1, keepdims=True))
           - jnp.exp(jnp.sum(lp[2:3] * lp[3:4], axis=1, keepdims=True)) + lam_init)
    l = sum(_piece_len(p) for p in k_of(0, 0))
    free = _free_cols(q0, l, k_true)
    tail = jnp.where(_admissible(q0, qn, free, l - free, k_true), 0.0, NEG)
    gain = g_ref[...] * (1.0 - lam_init)
    for h in range(C_HEADS):
        es, invs = [], []
        for part in range(2):
            cs = slice((2 * h + part) * HEAD_DIM, (2 * h + part + 1) * HEAD_DIM)
            s = _cat([_qk(q[:, cs], p) for p in k_of(h, part)])
            s = s + tail if free == 0 else jnp.concatenate([s[:, :free], s[:, free:] + tail], axis=1)
            m = s.max(axis=-1, keepdims=True)
            e = jnp.exp2(s - m)
            es.append(e)
            invs.append(1.0 / e.sum(axis=-1, keepdims=True))
        vp = v_of(h)
        o = _pv_pieces(es[0].astype(BF16), vp) * invs[0] - _pv_pieces(es[1].astype(BF16), vp) * (lam * invs[1])
        o = o * lax.rsqrt(jnp.mean(o * o, axis=-1, keepdims=True) + SUBLN_EPS) * gain
        o_ref[0, :, h * 2 * HEAD_DIM:(h + 1) * 2 * HEAD_DIM] = o.astype(o_ref.dtype)


def _diff_prompt_kernel(q0, k_len, lam_init, q_ref, kt_ref, v_ref, lam_ref, g_ref, o_ref):
    k_of = lambda h, part: [("T", kt_ref[0, (2 * h + part) * HEAD_DIM:(2 * h + part + 1) * HEAD_DIM, :])]
    v_of = lambda h: [("N", v_ref[0, :, h * 2 * HEAD_DIM:(h + 1) * 2 * HEAD_DIM])]
    _diff_attend(q0, k_len, lam_init, q_ref[0], k_of, v_of, lam_ref, g_ref, o_ref)


def _lam_init(layer):
    return 0.8 - 0.6 * math.exp(-0.3 * layer)


def _diff_prompt(cq, cktb, cvb, lam_p, subln_g, layer):
    b, s, _ = cq.shape
    qb = CAUSAL_QB
    small = [pl.BlockSpec((4, HEAD_DIM), lambda bi: (0, 0)), pl.BlockSpec((1, 2 * HEAD_DIM), lambda bi: (0, 0))]
    out = None
    for k in range(s // qb):
        k_len = (k + 1) * qb
        out = _call_into(
            functools.partial(_diff_prompt_kernel, k * qb, k_len, _lam_init(layer)), 5, None if out is None else (out,),
            out_shape=jax.ShapeDtypeStruct((b, s, C_WIDTH), BF16),
            grid=(b,),
            in_specs=[pl.BlockSpec((1, qb, C_WIDTH), functools.partial(lambda bi, k: (bi, k, 0), k=k)),
                      pl.BlockSpec((1, C_WIDTH, k_len), lambda bi: (bi, 0, 0)),
                      pl.BlockSpec((1, k_len, C_WIDTH), lambda bi: (bi, 0, 0))] + small,
            out_specs=pl.BlockSpec((1, qb, C_WIDTH), functools.partial(lambda bi, k: (bi, k, 0), k=k)),
            compiler_params=_params(("parallel",)),
            name="diff_prompt",
        )(cq, cktb, cvb, lam_p, subln_g)
    return out


def _diff_step_kernel(q0, k_true, lam_init, q_ref, pk_ref, pv_ref, k_ref, v_ref, lam_ref, g_ref, o_ref):
    p_len = pk_ref.shape[1] // C_HEADS
    kn = k_ref[0].astype(BF16)
    vn = v_ref[0].astype(BF16)

    def k_of(h, part):
        old = pk_ref[0, pl.ds(h, p_len, stride=C_HEADS), :].astype(BF16)
        cs = slice((2 * h + part) * HEAD_DIM, (2 * h + part + 1) * HEAD_DIM)
        return [("N", old[:, part * HEAD_DIM:(part + 1) * HEAD_DIM]), ("N", kn[:, cs])]

    def v_of(h):
        old = pv_ref[0, pl.ds(h, p_len, stride=C_HEADS), :].astype(BF16)
        return [("N", old), ("N", vn[:, h * 2 * HEAD_DIM:(h + 1) * 2 * HEAD_DIM])]

    _diff_attend(q0, k_true, lam_init, q_ref[0], k_of, v_of, lam_ref, g_ref, o_ref)


def _diff_step(cq, pck, pcv, ck, cv, lam_p, subln_g, layer, t_true, first):
    b, t, _ = cq.shape
    p_len = pck.shape[1] // C_HEADS
    assert p_len % LANES == 0 and ck.shape[1] % LANES == 0
    qspec = pl.BlockSpec((1, t, C_WIDTH), lambda bi: (bi, 0, 0))
    old = pl.BlockSpec((1, p_len * C_HEADS, LANES), lambda bi: (first + bi, 0, 0))
    new = pl.BlockSpec((1, ck.shape[1], C_WIDTH), lambda bi: (bi, 0, 0))
    return pl.pallas_call(
        functools.partial(_diff_step_kernel, p_len, p_len + t_true, _lam_init(layer)),
        out_shape=jax.ShapeDtypeStruct((b, t, C_WIDTH), BF16),
        grid=(b,),
        in_specs=[qspec, old, old, new, new,
                  pl.BlockSpec((4, HEAD_DIM), lambda bi: (0, 0)), pl.BlockSpec((1, 2 * HEAD_DIM), lambda bi: (0, 0))],
        out_specs=qspec,
        compiler_params=_params(("parallel",)),
        name="diff_step",
    )(cq, pck, pcv, ck, cv, lam_p, subln_g)


def _merge_kernel(x_ref, ya_ref, yb_ref, yc_ref, wg_ref, wa_ref, wb_ref, wc_ref, wo_ref, g_ref, b_ref, o_ref):
    x = x_ref[...]
    xb = x.astype(BF16)
    merged = None
    for k, (y_ref, w_ref) in enumerate(((ya_ref, wa_ref), (yb_ref, wb_ref), (yc_ref, wc_ref))):
        gate = jax.nn.sigmoid(_dot(xb, wg_ref[:, k * D_MODEL:(k + 1) * D_MODEL]))
        term = gate * _dot(y_ref[...], w_ref[...])
        merged = term if merged is None else merged + term
    out = _dot(merged.astype(BF16), wo_ref[...])
    o_ref[...] = _layer_norm(ALPHA * x + out, g_ref[...], b_ref[...])


def _merge(x, ya, yb, yc, w_gates, wb_a, wb_b, wb_c, w_out, g, b):
    n = x.shape[0]
    tm = _row_tile(n, 512)
    row = lambda w: pl.BlockSpec((tm, w), lambda i: (i, 0))
    full = lambda a: pl.BlockSpec(a.shape, lambda i: (0, 0))
    return pl.pallas_call(
        _merge_kernel,
        out_shape=jax.ShapeDtypeStruct((n, D_MODEL), F32),
        grid=(n // tm,),
        in_specs=[row(D_MODEL), row(A_WIDTH), row(B_WIDTH), row(C_WIDTH),
                  full(w_gates), full(wb_a), full(wb_b), full(wb_c), full(w_out), full(g), full(b)],
        out_specs=row(D_MODEL),
        compiler_params=_params(("parallel",)),
        name="merge",
    )(x, ya, yb, yc, w_gates, wb_a, wb_b, wb_c, w_out, g, b)


MEM_TILES = MEM_HEAD_DIM // LANES
MEM_ROWS = MEM_HEADS * MEM_TILES


def _mem_kernel(x_ref, mk_ref, mv_ref, wq_ref, wo_ref, g_ref, b_ref, o_ref, mkb_ref, mvb_ref):
    i = pl.program_id(1)

    @pl.when(i == 0)
    def _():
        for h in range(MEM_HEADS):
            for c in range(MEM_TILES):
                rows = pl.ds(c * MEM_HEADS + h, N_MEM, stride=MEM_ROWS)
                mkb_ref[h, :, c * LANES:(c + 1) * LANES] = mk_ref[0, rows, :].astype(BF16)
                mvb_ref[h, :, c * LANES:(c + 1) * LANES] = mv_ref[0, rows, :].astype(BF16)

    x = x_ref[0]
    q = _dot(x.astype(BF16), wq_ref[...]).astype(BF16)
    scale = MEM_HEAD_DIM ** -0.5
    heads = []
    for h in range(MEM_HEADS):
        hs = slice(h * MEM_HEAD_DIM, (h + 1) * MEM_HEAD_DIM)
        s = _dot_nt(q[:, hs], mkb_ref[h]) * scale
        m = s.max(axis=-1, keepdims=True)
        p = jnp.exp(s - m)
        den = p.sum(axis=-1, keepdims=True)
        heads.append((_dot(p.astype(BF16), mvb_ref[h]) / den).astype(BF16))
    o = _dot(jnp.concatenate(heads, axis=1), wo_ref[...])
    o_ref[0] = _layer_norm(ALPHA * x + o, g_ref[...], b_ref[...])


def _mem_attn(x, mem_k, mem_v, first, w_q, w_o, g, b):
    bsz, t, _ = x.shape
    tq = _row_tile(t, 512)
    xspec = pl.BlockSpec((1, tq, D_MODEL), lambda bi, i: (bi, i, 0))
    mspec = pl.BlockSpec((1, N_MEM * MEM_ROWS, LANES), lambda bi, i: (first + bi, 0, 0))
    full = lambda a: pl.BlockSpec(a.shape, lambda bi, i: (0, 0))
    return pl.pallas_call(
        _mem_kernel,
        out_shape=jax.ShapeDtypeStruct((bsz, t, D_MODEL), F32),
        grid=(bsz, t // tq),
        in_specs=[xspec, mspec, mspec, full(w_q), full(w_o), full(g), full(b)],
        out_specs=xspec,
        scratch_shapes=[pltpu.VMEM((MEM_HEADS, N_MEM, MEM_HEAD_DIM), BF16)] * 2,
        compiler_params=_params(("parallel", "arbitrary")),
        name="mem_attn",
    )(x, mem_k, mem_v, w_q, w_o, g, b)


def _store_mem_rows(o_ref, v):
    rows = v.shape[0]
    for h in range(MEM_HEADS):
        for c in range(MEM_TILES):
            col = h * MEM_HEAD_DIM + c * LANES
            o_ref[pl.ds(c * MEM_HEADS + h, rows, stride=MEM_ROWS), :] = v[:, col:col + LANES]


def _memproj_kernel(x_ref, wk_ref, wv_ref, k_ref, v_ref):
    xb = x_ref[...].astype(BF16)
    _store_mem_rows(k_ref, _dot(xb, wk_ref[...]))
    _store_mem_rows(v_ref, _dot(xb, wv_ref[...]))


def _memproj(mem, w_k, w_v, layer, depth, prev):
    n = mem.shape[0]
    tm = _row_tile(n, 512)
    nblk = n // tm
    row = pl.BlockSpec((tm, D_MODEL), lambda i: (i, 0))
    full = pl.BlockSpec((D_MODEL, D_MODEL), lambda i: (0, 0))
    out = pl.BlockSpec((tm * MEM_ROWS, LANES), lambda i: (layer * nblk + i, 0))
    return _call_into(
        _memproj_kernel, 3, prev,
        out_shape=(jax.ShapeDtypeStruct((depth * n * MEM_ROWS, LANES), F32),) * 2,
        grid=(n // tm,),
        in_specs=[row, full, full],
        out_specs=(out, out),
        compiler_params=_params(("parallel",)),
        name="memproj",
    )(mem, w_k, w_v)


def _pad_rows(a, rows):
    pad = rows - a.shape[1]
    return a if pad == 0 else jnp.pad(a, ((0, 0), (0, pad), (0, 0)))


def _rest_of_layer(x2, ya, yb, yc, mem_k, mem_v, mem_first, b, t, w):
    n = b * t
    x3 = _merge(x2, ya.reshape(n, -1), yb.reshape(n, -1), yc.reshape(n, -1),
                w["gates"], w["wb_a"], w["wb_b"], w["wb_c"], w["w_out"], w["ln_g"][1:2], w["ln_b"][1:2])
    x4 = _mem_attn(x3.reshape(b, t, D_MODEL), mem_k, mem_v, mem_first, w["wm_q"], w["wm_o"],
                   w["ln_g"][2:3], w["ln_b"][2:3])
    x5 = _ffn(x4.reshape(n, D_MODEL), w["f2_gu"], w["f2_d"], w["ln_g"][3:4], w["ln_b"][3:4])
    return x5.reshape(b, t, D_MODEL)


def _prompt_layer(x, mem_k, mem_v, layer, depth, prev_c, w):
    b, s, _ = x.shape
    n = b * s
    x2 = _ffn(x.reshape(n, D_MODEL), w["f1_gu"], w["f1_d"], w["ln_g"][0:1], w["ln_b"][0:1])
    tab = _rope_tables(jnp.arange(s, dtype=jnp.int32))
    (ck4, cv4, aq, akt, avt, aktb, avtb, bqt, bkt, bvt, ikt, bkb, bvtb, ikb,
     iqt, iwt, cq, cktb, cvb) = _inproj_prompt(x2, w["in_pack"], tab, b, s, layer, depth, prev_c)
    seq = lambda a: a.reshape(b, s, a.shape[-1])
    ya = _band_prompt(seq(aq), aktb, avtb, w["rel_bias"])
    yb = _dsa_prompt(iqt, iwt, bqt, seq(ikb), seq(bkb), bvtb)
    yc = _diff_prompt(seq(cq), cktb, seq(cvb), w["lam"], w["subln_g"], layer)
    y = _rest_of_layer(x2, ya, yb, yc, mem_k, mem_v, layer * b, b, s, w)
    keep = min(A_WINDOW, s)
    heads_last = lambda a: jnp.transpose(a[:, :, s - keep:].reshape(b, A_HEADS, HEAD_DIM, keep), (0, 3, 1, 2))
    rows_last = lambda a: jnp.swapaxes(a, 1, 2)
    state = (heads_last(akt), heads_last(avt), rows_last(bkt), rows_last(bvt), rows_last(ikt))
    return y, state, (ck4, cv4)


def _step_layer(x, p_len, past, mem_k, mem_v, layer, w):
    b, t, _ = x.shape
    first = layer * b
    n = b * t
    x2 = _ffn(x.reshape(n, D_MODEL), w["f1_gu"], w["f1_d"], w["ln_g"][0:1], w["ln_b"][0:1])
    tm = _row_tile(n, INPROJ_TM)
    assert tm % t == 0
    tab = jnp.tile(_rope_tables(p_len + jnp.arange(t, dtype=jnp.int32)), (1, 1, tm // t, 1))
    outs = _inproj_step(x2, w["in_pack"], tab)
    aq, ak, av, bq, bk, bv, iq, ik, iw, cq, ck, cv = [o.reshape(b, t, o.shape[-1]) for o in outs]
    pa_k, pa_v, pb_k, pb_v, pb_i, pc_k, pc_v = past
    ya = _band_step(aq, ak, av, pa_k, pa_v, w["rel_bias"], first)
    t_pad = -(-t // LANES) * LANES
    pad = lambda a: _pad_rows(a, t_pad)
    yb = _dsa_step(iq, iw, bq, pb_i, pb_k, pb_v, pad(ik), pad(bk), pad(bv), t, first)
    yc = _diff_step(cq, pc_k, pc_v, pad(ck), pad(cv), w["lam"], w["subln_g"], layer, t, first)
    y = _rest_of_layer(x2, ya, yb, yc, mem_k, mem_v, first, b, t, w)
    state = (ak.reshape(b, t, A_HEADS, HEAD_DIM), av.reshape(b, t, A_HEADS, HEAD_DIM), bk, bv, ik,
             ck.reshape(b, t, C_HEADS, 2 * HEAD_DIM), cv.reshape(b, t, C_HEADS, 2 * HEAD_DIM))
    return y, state


def _layer_weights(l, ln_g, ln_b, ffn1_w_gu, ffn1_w_d, ffn2_w_gu, ffn2_w_d, w_in, a_rel_bias, c_lambda,
                   c_subln_g, w_branch_a, w_branch_b, w_branch_c, w_out, w_mem_q, w_mem_k, w_mem_v, w_mem_o):
    wi = w_in[l]
    iw_end = IN_OFFS[9]
    gates_at = IN_OFFS[12]
    in_pack = jnp.concatenate(
        [wi[:, :iw_end], jnp.zeros((D_MODEL, IW_PAD), wi.dtype), wi[:, iw_end:gates_at]], axis=1)
    bf = lambda a: a.astype(BF16)
    return dict(
        ln_g=ln_g[l].astype(F32), ln_b=ln_b[l].astype(F32),
        f1_gu=bf(ffn1_w_gu[l]), f1_d=bf(ffn1_w_d[l]), f2_gu=bf(ffn2_w_gu[l]), f2_d=bf(ffn2_w_d[l]),
        in_pack=bf(in_pack), gates=bf(wi[:, gates_at:]),
        rel_bias=a_rel_bias[l], lam=c_lambda[l].astype(F32), subln_g=c_subln_g[l].astype(F32)[None, :],
        wb_a=bf(w_branch_a[l]), wb_b=bf(w_branch_b[l]), wb_c=bf(w_branch_c[l]), w_out=bf(w_out[l]),
        wm_q=bf(w_mem_q[l]), wm_k=bf(w_mem_k[l]), wm_v=bf(w_mem_v[l]), wm_o=bf(w_mem_o[l]),
    )


def kernel(x_prompt, x_sample, cache_a_k, cache_a_v, cache_b_k, cache_b_v, cache_b_idx, cache_c_k, cache_c_v, cache_mem_k, cache_mem_v, mem_prompt, ln_g, ln_b, ffn1_w_gu, ffn1_w_d, ffn2_w_gu, ffn2_w_d, w_in, a_rel_bias, c_lambda, c_subln_g, w_branch_a, w_branch_b, w_branch_c, w_out, w_mem_q, w_mem_k, w_mem_v, w_mem_o):
    bp, s, _ = x_prompt.shape
    bs, t, _ = x_sample.shape
    p_len = cache_b_k.shape[2]
    depth = ln_g.shape[0]
    yp, ys = x_prompt, x_sample
    st_p, st_s = [], []
    rows = depth * bs
    keys_last = lambda a: jnp.transpose(a, (0, 1, 3, 2)).reshape(rows, a.shape[3], a.shape[2])
    pairs = lambda a: a.reshape(rows, a.shape[2] * C_HEADS, 2 * HEAD_DIM)
    past = (cache_a_k.reshape(rows, -1, A_WIDTH), cache_a_v.reshape(rows, -1, A_WIDTH),
            keys_last(cache_b_k), keys_last(cache_b_v), keys_last(cache_b_idx), pairs(cache_c_k), pairs(cache_c_v))
    mem_rows = lambda a: a.reshape(rows, N_MEM, MEM_HEADS, MEM_TILES, LANES).transpose(0, 1, 3, 2, 4).reshape(
        rows, N_MEM * MEM_ROWS, LANES)
    mem_k_s, mem_v_s = mem_rows(cache_mem_k), mem_rows(cache_mem_v)
    mem_p = c_p = None
    for l in range(depth):
        w = _layer_weights(l, ln_g, ln_b, ffn1_w_gu, ffn1_w_d, ffn2_w_gu, ffn2_w_d, w_in, a_rel_bias,
                           c_lambda, c_subln_g, w_branch_a, w_branch_b, w_branch_c, w_out,
                           w_mem_q, w_mem_k, w_mem_v, w_mem_o)
        mem_p = _memproj(mem_prompt.reshape(bp * N_MEM, D_MODEL), w["wm_k"], w["wm_v"], l, depth, mem_p)
        mk, mv = (a.reshape(depth * bp, N_MEM * MEM_ROWS, LANES) for a in mem_p)
        yp, sp, c_p = _prompt_layer(yp, mk, mv, l, depth, c_p, w)
        ys, ss = _step_layer(ys, p_len, past, mem_k_s, mem_v_s, l, w)
        st_p.append(sp)
        st_s.append(ss)
    stk = lambda sts: [jnp.stack([st[i] for st in sts]) for i in range(len(sts[0]))]
    (a_k_p, a_v_p, b_k_p, b_v_p, b_i_p) = stk(st_p)
    (a_k_s, a_v_s, b_k_s, b_v_s, b_i_s, c_k_s, c_v_s) = stk(st_s)
    c_k_p, c_v_p = (a.reshape(depth, bp, s, C_HEADS, 2 * HEAD_DIM) for a in c_p)
    mem_out = lambda a: a.reshape(depth, bp, N_MEM, MEM_TILES, MEM_HEADS, LANES).transpose(0, 1, 2, 4, 3, 5).reshape(
        depth, bp, N_MEM, MEM_HEADS, MEM_HEAD_DIM)
    return (yp, ys, a_k_p, a_v_p, b_k_p, b_v_p, b_i_p, c_k_p, c_v_p,
            mem_out(mem_p[0]), mem_out(mem_p[1]), a_k_s, a_v_s, b_k_s, b_v_s, b_i_s, c_k_s, c_v_s)
```

```python
import functools
import math

import jax
import jax.numpy as jnp
import numpy as np
from jax import lax
from jax.experimental import pallas as pl
from jax.experimental.pallas import tpu as pltpu

F32 = jnp.float32
BF16 = jnp.bfloat16

D_MODEL = 1024
CHUNK = 64
HEAD_DIM = 64
ROT_DIM = HEAD_DIM // 4
ROT_HALF = ROT_DIM // 2
ROPE_THETA = 500000.0
A_HEADS = 4
A_LEFT_CHUNKS = 8
A_WINDOW = A_LEFT_CHUNKS * CHUNK
REL_CLIP = 128
B_HEADS = 4
IDX_HEADS = 8
IDX_DIM = 64
TOPK_MAX = 256
C_HEADS = 4
N_MEM = 256
MEM_HEADS = 4
MEM_HEAD_DIM = D_MODEL // MEM_HEADS
D_FF = ((8 * D_MODEL // 3 + 127) // 128) * 128
N_BRANCH = 3
DEPTH = 2
ALPHA = (2.0 * DEPTH) ** 0.25
LN_EPS = 1e-5
SUBLN_EPS = 1e-5
A_WIDTH = A_HEADS * HEAD_DIM
B_WIDTH = B_HEADS * HEAD_DIM
C_WIDTH = C_HEADS * 2 * HEAD_DIM
IN_SIZES = (A_WIDTH, A_WIDTH, A_WIDTH,
            B_WIDTH, HEAD_DIM, HEAD_DIM, IDX_HEADS * IDX_DIM, IDX_DIM, IDX_HEADS,
            C_WIDTH, C_WIDTH, C_WIDTH,
            N_BRANCH * D_MODEL)
IN_OFFS = tuple(int(v) for v in np.cumsum((0,) + IN_SIZES))

LANES = 128
V7X_VMEM_LIMIT = 56 * 1024 * 1024

NEG = -1e30
LOG2E = math.log2(math.e)
INT_MIN = -2 ** 31

IW_PAD = LANES - HEAD_DIM - IDX_HEADS
P_AQ, P_AK, P_AV = 0, 256, 512
P_B = 768
P_I = 1152
P_C = 1792
P_END = 3328


def _dot(a, b):
    return jnp.dot(a, b, preferred_element_type=F32)


def _dot_nt(a, b):
    return lax.dot_general(a, b, (((1,), (1,)), ((), ())), preferred_element_type=F32)


def _layer_norm(v, g, b):
    mu = jnp.mean(v, axis=-1, keepdims=True)
    d = v - mu
    var = jnp.mean(d * d, axis=-1, keepdims=True)
    return d * lax.rsqrt(var + LN_EPS) * g + b


def _params(sem):
    return pltpu.CompilerParams(dimension_semantics=sem, vmem_limit_bytes=V7X_VMEM_LIMIT)


def _row_tile(n, want):
    t = min(n, want)
    assert n % t == 0
    return t


def _call_into(body, n_in, prev, **kw):
    if prev is None:
        return pl.pallas_call(body, **kw)
    n_prev = len(prev)
    kernel = lambda *refs: body(*refs[:n_in], *refs[n_in + n_prev:])
    kw["in_specs"] = list(kw["in_specs"]) + [pl.BlockSpec(memory_space=pl.ANY)] * n_prev
    call = pl.pallas_call(kernel, input_output_aliases={n_in + i: i for i in range(n_prev)}, **kw)
    return lambda *args: call(*args, *prev)


def _ffn_kernel(x_ref, wg_ref, wu_ref, wd_ref, g_ref, b_ref, o_ref, xb_ref, acc_ref, pend_ref):
    i = pl.program_id(0)
    j = pl.program_id(1)
    nt = pl.num_programs(0) - 1
    last = pl.num_programs(1) - 1

    def chunk():
        xb = xb_ref[...]
        g = _dot(xb, wg_ref[...])
        u = _dot(xb, wu_ref[...])
        h = g * jax.nn.sigmoid(g) * u
        return _dot(h.astype(BF16), wd_ref[...])

    def finish_previous():
        o_ref[...] = _layer_norm(pend_ref[...], g_ref[...], b_ref[...])

    @pl.when((j == 0) & (i == 0))
    def _():
        xb_ref[...] = x_ref[...].astype(BF16)
        acc_ref[...] = chunk()

    @pl.when((j == 0) & (i > 0) & (i < nt))
    def _():
        xb_ref[...] = x_ref[...].astype(BF16)
        acc_ref[...] = chunk()
        finish_previous()

    @pl.when((j == 0) & (i == nt))
    def _():
        finish_previous()

    @pl.when((j > 0) & (j < last) & (i < nt))
    def _():
        acc_ref[...] += chunk()

    @pl.when((j == last) & (i < nt))
    def _():
        pend_ref[...] = ALPHA * x_ref[...] + 0.5 * (acc_ref[...] + chunk())


def _ffn(x, w_gu, w_d, g, b):
    n = x.shape[0]
    tm = _row_tile(n, 1024)
    tf = 256
    nf = D_FF // tf
    nt = n // tm
    assert nf >= 2
    return pl.pallas_call(
        _ffn_kernel,
        out_shape=jax.ShapeDtypeStruct((n, D_MODEL), F32),
        grid=(nt + 1, nf),
        in_specs=[
            pl.BlockSpec((tm, D_MODEL), lambda i, j: (jnp.minimum(i, nt - 1), 0)),
            pl.BlockSpec((D_MODEL, tf), lambda i, j: (0, j)),
            pl.BlockSpec((D_MODEL, tf), lambda i, j: (0, j + nf)),
            pl.BlockSpec((tf, D_MODEL), lambda i, j: (j, 0)),
            pl.BlockSpec((1, D_MODEL), lambda i, j: (0, 0)),
            pl.BlockSpec((1, D_MODEL), lambda i, j: (0, 0)),
        ],
        out_specs=pl.BlockSpec((tm, D_MODEL), lambda i, j: (jnp.maximum(i - 1, 0), 0)),
        scratch_shapes=[pltpu.VMEM((tm, D_MODEL), BF16), pltpu.VMEM((tm, D_MODEL), F32),
                        pltpu.VMEM((tm, D_MODEL), F32)],
        compiler_params=_params(("arbitrary", "arbitrary")),
        name="ffn",
    )(x, w_gu, w_gu, w_d, g, b)


def _rope_tables(pos):
    t = pos.shape[0]
    inv_freq = ROPE_THETA ** (-jnp.arange(ROT_HALF, dtype=F32) / ROT_HALF)
    ang = pos.astype(F32)[:, None] * inv_freq
    cos, sin = jnp.cos(ang), jnp.sin(ang)
    rest = HEAD_DIM - ROT_DIM
    ones = lambda w: jnp.ones((t, w), F32)
    zeros = lambda w: jnp.zeros((t, w), F32)
    c64 = jnp.concatenate([cos, cos, ones(rest)], axis=1)
    lo64 = jnp.concatenate([-sin, zeros(ROT_HALF), zeros(rest)], axis=1)
    hi64 = jnp.concatenate([zeros(ROT_HALF), sin, zeros(rest)], axis=1)
    iw64 = jnp.concatenate([jnp.full((t, IDX_HEADS), IDX_HEADS ** -0.5, F32), ones(HEAD_DIM - IDX_HEADS)], axis=1)
    cat = lambda a, b_: jnp.concatenate([a, b_], axis=1)
    both = jnp.stack([cat(c64, c64), cat(lo64, lo64), cat(hi64, hi64)])
    upper_id = jnp.stack([cat(c64, ones(HEAD_DIM)), cat(lo64, zeros(HEAD_DIM)), cat(hi64, zeros(HEAD_DIM))])
    upper_iw = jnp.stack([cat(c64, iw64), cat(lo64, zeros(HEAD_DIM)), cat(hi64, zeros(HEAD_DIM))])
    return jnp.stack([both, upper_id, upper_iw])


def _rope_apply(z, tab_ref, variant):
    cos = tab_ref[variant, 0]
    lo = tab_ref[variant, 1]
    hi = tab_ref[variant, 2]
    outs = []
    for k in range(z.shape[1] // LANES):
        zc = z[:, k * LANES:(k + 1) * LANES]
        outs.append(zc * cos + pltpu.roll(zc, LANES - ROT_HALF, 1) * lo + pltpu.roll(zc, ROT_HALF, 1) * hi)
    return outs[0] if len(outs) == 1 else jnp.concatenate(outs, axis=1)


def _inproj_sections(x_ref, w_ref, tab_ref):
    xb = x_ref[...].astype(BF16)
    za = _dot(xb, w_ref[:, P_AQ:P_B])
    zb = _dot(xb, w_ref[:, P_B:P_I])
    zi = _dot(xb, w_ref[:, P_I:P_C])
    zc = _dot(xb, w_ref[:, P_C:P_END])
    qs = HEAD_DIM ** -0.5 * LOG2E
    return dict(
        aq=za[:, 0:256] * qs, ak=za[:, 256:512], av=za[:, 512:768],
        bq=_rope_apply(zb[:, 0:256], tab_ref, 0) * qs,
        bkv=_rope_apply(zb[:, 256:384], tab_ref, 1),
        iq=_rope_apply(zi[:, 0:512], tab_ref, 0),
        ikw=_rope_apply(zi[:, 512:640], tab_ref, 2),
        cq=_rope_apply(zc[:, 0:512], tab_ref, 0) * qs,
        ck=_rope_apply(zc[:, 512:1024], tab_ref, 0),
        cv=zc[:, 1024:1536])


def _store_heads_interleaved(o_ref, v, heads):
    rows = v.shape[0]
    for h in range(heads):
        o_ref[pl.ds(h, rows, stride=heads), :] = v[:, h * LANES:(h + 1) * LANES]


def _inproj_prompt_kernel(x_ref, w_ref, tab_ref, ck4_ref, cv4_ref,
                          aq_ref, akt_ref, avt_ref, aktb_ref, avtb_ref,
                          bqt_ref, bkt_ref, bvt_ref, ikt_ref, bkb_ref, bvtb_ref, ikb_ref,
                          iqt_ref, iwt_ref, cqt_ref, ckb_ref, cvtb_ref):
    s = _inproj_sections(x_ref, w_ref, tab_ref)
    aq_ref[...] = s["aq"].astype(BF16)
    akt = s["ak"].T
    avt = s["av"].T
    akt_ref[0] = akt
    avt_ref[0] = avt
    aktb_ref[0] = akt.astype(BF16)
    avtb_ref[0] = avt.astype(BF16)
    bqt_ref[0] = s["bq"].T.astype(BF16)
    kvt = s["bkv"].T
    bkt_ref[0] = kvt[0:HEAD_DIM]
    bvt_ref[0] = kvt[HEAD_DIM:2 * HEAD_DIM]
    bkb_ref[...] = s["bkv"][:, 0:HEAD_DIM].astype(BF16)
    bvtb_ref[0] = kvt[HEAD_DIM:2 * HEAD_DIM].astype(BF16)
    kwt = s["ikw"].T
    ikt_ref[0] = kwt[0:IDX_DIM]
    ikb_ref[...] = s["ikw"][:, 0:IDX_DIM].astype(BF16)
    iqt_ref[0] = s["iq"].T.astype(BF16)
    iwt_ref[0] = kwt[IDX_DIM:IDX_DIM + IDX_HEADS]
    cqt_ref[0] = s["cq"].T.astype(BF16)
    _store_heads_interleaved(ck4_ref, s["ck"], C_HEADS)
    _store_heads_interleaved(cv4_ref, s["cv"], C_HEADS)
    ckb_ref[...] = s["ck"].astype(BF16)
    cvtb_ref[0] = s["cv"].T.astype(BF16)


INPROJ_TM = 256


def _inproj_prompt(x, w_pack, tab, bsz, seq, layer, depth, prev):
    n = x.shape[0]
    tm = INPROJ_TM
    spb = seq // tm
    nblk = n // tm
    tok = lambda w: pl.BlockSpec((tm, w), lambda i: (i, 0))
    tr = lambda d: pl.BlockSpec((1, d, tm), lambda i: (i // spb, 0, i % spb))
    sds = jax.ShapeDtypeStruct
    tok_o = lambda w, dt: (sds((n, w), dt), tok(w))
    tr_o = lambda d, dt: (sds((bsz, d, seq), dt), tr(d))
    il_o = (sds((depth * n * C_HEADS, LANES), F32),
            pl.BlockSpec((tm * C_HEADS, LANES), lambda i: (layer * nblk + i, 0)))
    outs = [il_o, il_o,
            tok_o(A_WIDTH, BF16), tr_o(A_WIDTH, F32), tr_o(A_WIDTH, F32), tr_o(A_WIDTH, BF16), tr_o(A_WIDTH, BF16),
            tr_o(B_WIDTH, BF16), tr_o(HEAD_DIM, F32), tr_o(HEAD_DIM, F32), tr_o(IDX_DIM, F32),
            tok_o(HEAD_DIM, BF16), tr_o(HEAD_DIM, BF16), tok_o(IDX_DIM, BF16),
            tr_o(IDX_HEADS * IDX_DIM, BF16), tr_o(IDX_HEADS, F32), tr_o(C_WIDTH, BF16),
            tok_o(C_WIDTH, BF16), tr_o(C_WIDTH, BF16)]
    return _call_into(
        _inproj_prompt_kernel, 3, prev,
        out_shape=tuple(o[0] for o in outs),
        grid=(n // tm,),
        in_specs=[tok(D_MODEL),
                  pl.BlockSpec((D_MODEL, P_END), lambda i: (0, 0)),
                  pl.BlockSpec((3, 3, tm, LANES), lambda i: (0, 0, i % spb, 0))],
        out_specs=tuple(o[1] for o in outs),
        compiler_params=_params(("parallel",)),
        name="inproj_prompt",
    )(x, w_pack, tab)


def _inproj_step_kernel(x_ref, w_ref, tab_ref,
                        aq_ref, ak_ref, av_ref, bq_ref, bk_ref, bv_ref,
                        iq_ref, ik_ref, iw_ref, cq_ref, ck_ref, cv_ref):
    s = _inproj_sections(x_ref, w_ref, tab_ref)
    aq_ref[...] = s["aq"].astype(BF16)
    ak_ref[...] = s["ak"]
    av_ref[...] = s["av"]
    bq_ref[...] = s["bq"].astype(BF16)
    bk_ref[...] = s["bkv"][:, 0:HEAD_DIM]
    bv_ref[...] = s["bkv"][:, HEAD_DIM:2 * HEAD_DIM]
    iq_ref[...] = s["iq"].astype(BF16)
    ik_ref[...] = s["ikw"][:, 0:IDX_DIM]
    iw_ref[...] = s["ikw"][:, IDX_DIM:IDX_DIM + IDX_HEADS]
    cq_ref[...] = s["cq"].astype(BF16)
    ck_ref[...] = s["ck"]
    cv_ref[...] = s["cv"]


def _inproj_step(x, w_pack, tab):
    n = x.shape[0]
    tm = _row_tile(n, INPROJ_TM)
    widths = ((A_WIDTH, BF16), (A_WIDTH, F32), (A_WIDTH, F32), (B_WIDTH, BF16), (HEAD_DIM, F32), (HEAD_DIM, F32),
              (IDX_HEADS * IDX_DIM, BF16), (IDX_DIM, F32), (IDX_HEADS, F32),
              (C_WIDTH, BF16), (C_WIDTH, F32), (C_WIDTH, F32))
    return pl.pallas_call(
        _inproj_step_kernel,
        out_shape=tuple(jax.ShapeDtypeStruct((n, w), dt) for w, dt in widths),
        grid=(n // tm,),
        in_specs=[
            pl.BlockSpec((tm, D_MODEL), lambda i: (i, 0)),
            pl.BlockSpec((D_MODEL, P_END), lambda i: (0, 0)),
            pl.BlockSpec((3, 3, tm, LANES), lambda i: (0, 0, 0, 0)),
        ],
        out_specs=tuple(pl.BlockSpec((tm, w), lambda i: (i, 0)) for w, _ in widths),
        compiler_params=_params(("parallel",)),
        name="inproj_step",
    )(x, w_pack, tab)


def _qk(q, piece):
    kind, k = piece
    return _dot(q, k) if kind == "T" else _dot_nt(q, k)


def _pv(p, piece):
    kind, v = piece
    return _dot_nt(p, v) if kind == "T" else _dot(p, v)


def _piece_len(piece):
    kind, a = piece
    return a.shape[1] if kind == "T" else a.shape[0]


def _cat(parts):
    return parts[0] if len(parts) == 1 else jnp.concatenate(parts, axis=1)


def _pv_pieces(p, pieces):
    off = 0
    o = None
    for piece in pieces:
        n = _piece_len(piece)
        t = _pv(p[:, off:off + n], piece)
        o = t if o is None else o + t
        off += n
    return o


def _toeplitz_bias(rel_bias, q_rows, k_cols, q_off):
    period = pl.next_power_of_2(q_rows + k_cols)
    e = np.arange(period)
    e = np.where(e >= k_cols, e - period, e)
    idx = np.clip(q_off - e, -REL_CLIP, REL_CLIP) + REL_CLIP
    table = rel_bias[:, idx].astype(F32) * LOG2E
    flat = jnp.tile(table, (1, q_rows + 1))[:, :q_rows * (period - 1)]
    return flat.reshape(rel_bias.shape[0], q_rows, period - 1)[:, :, :k_cols]


def _band_attend(q, pieces, bias_ref, o_ref):
    for h in range(A_HEADS):
        hs = slice(h * HEAD_DIM, (h + 1) * HEAD_DIM)
        qh = q[:, hs]
        ss = []
        off = 0
        for k_of, _, extra in pieces:
            kp = k_of(h)
            n = _piece_len(kp)
            s = _qk(qh, kp) + bias_ref[h, :, off:off + n]
            if extra is not None:
                s = s + extra
            ss.append(s)
            off += n
        m = ss[0].max(axis=-1, keepdims=True)
        for s in ss[1:]:
            m = jnp.maximum(m, s.max(axis=-1, keepdims=True))
        l = jnp.zeros_like(m)
        o = jnp.zeros((q.shape[0], HEAD_DIM), F32)
        for s, (_, v_of, _) in zip(ss, pieces):
            p = jnp.exp2(s - m)
            l = l + p.sum(axis=-1, keepdims=True)
            o = o + _pv(p.astype(BF16), v_of(h))
        o_ref[0, :, hs] = (o / l).astype(o_ref.dtype)


BAND_QB = 256
BAND_KB = 3


def _band_prompt_kernel(q_ref, k0_ref, k1_ref, k2_ref, v0_ref, v1_ref, v2_ref, bias_ref, o_ref):
    j = pl.program_id(1)
    pieces = []
    for kb, (k_ref, v_ref) in enumerate(((k0_ref, v0_ref), (k1_ref, v1_ref), (k2_ref, v2_ref))):
        back = BAND_KB - 1 - kb
        extra = None if back == 0 else jnp.where(j >= back, 0.0, NEG)
        head_rows = lambda ref: (lambda h: ("T", ref[0, h * HEAD_DIM:(h + 1) * HEAD_DIM, :]))
        pieces.append((head_rows(k_ref), head_rows(v_ref), extra))
    _band_attend(q_ref[0], pieces, bias_ref, o_ref)


def _band_prompt(aq, akt, avt, rel_bias):
    b, s, _ = aq.shape
    assert A_WINDOW == (BAND_KB - 1) * BAND_QB and s % BAND_QB == 0
    kw = BAND_KB * BAND_QB
    r = np.arange(BAND_QB)[:, None] + A_WINDOW
    c = np.arange(kw)[None, :]
    in_band = (c // CHUNK <= r // CHUNK) & (c // CHUNK >= r // CHUNK - A_LEFT_CHUNKS)
    bias = jnp.where(in_band[None], _toeplitz_bias(rel_bias, BAND_QB, kw, A_WINDOW), NEG)
    qspec = pl.BlockSpec((1, BAND_QB, A_WIDTH), lambda bi, j: (bi, j, 0))
    kspecs = [pl.BlockSpec((1, A_WIDTH, BAND_QB),
                           functools.partial(lambda bi, j, back: (bi, 0, jnp.maximum(j - back, 0)),
                                             back=BAND_KB - 1 - kb))
              for kb in range(BAND_KB)]
    return pl.pallas_call(
        _band_prompt_kernel,
        out_shape=jax.ShapeDtypeStruct((b, s, A_WIDTH), BF16),
        grid=(b, s // BAND_QB),
        in_specs=[qspec] + kspecs + kspecs + [pl.BlockSpec((A_HEADS, BAND_QB, kw), lambda bi, j: (0, 0, 0))],
        out_specs=qspec,
        compiler_params=_params(("parallel", "parallel")),
        name="band_prompt",
    )(aq, akt, akt, akt, avt, avt, avt, bias)


def _band_step_kernel(q_ref, pk_ref, k_ref, pv_ref, v_ref, bias_ref, o_ref):
    head_cols = lambda a: (lambda h: ("N", a[:, h * HEAD_DIM:(h + 1) * HEAD_DIM]))
    pieces = [(head_cols(pk_ref[0].astype(BF16)), head_cols(pv_ref[0].astype(BF16)), None),
              (head_cols(k_ref[0].astype(BF16)), head_cols(v_ref[0].astype(BF16)), None)]
    _band_attend(q_ref[0], pieces, bias_ref, o_ref)


def _band_step(aq, ak, av, pk, pv, rel_bias, first):
    b, t, _ = aq.shape
    w = pk.shape[1]
    bias = _toeplitz_bias(rel_bias, t, w + t, w)
    new = pl.BlockSpec((1, t, A_WIDTH), lambda bi: (bi, 0, 0))
    old = pl.BlockSpec((1, w, A_WIDTH), lambda bi: (first + bi, 0, 0))
    return pl.pallas_call(
        _band_step_kernel,
        out_shape=jax.ShapeDtypeStruct((b, t, A_WIDTH), BF16),
        grid=(b,),
        in_specs=[new, old, new, old, new, pl.BlockSpec((A_HEADS, t, w + t), lambda bi: (0, 0, 0))],
        out_specs=new,
        compiler_params=_params(("parallel",)),
        name="band_step",
    )(aq, pk, ak, pv, av, bias)


def _admissible(q0, q_rows, k0, k_cols, k_true):
    kpos = lax.broadcasted_iota(jnp.int32, (q_rows, k_cols), 1) + k0
    qpos = lax.broadcasted_iota(jnp.int32, (q_rows, k_cols), 0) + q0
    shift = CHUNK.bit_length() - 1
    return ((kpos >> shift) <= (qpos >> shift)) & (kpos < k_true)


def _free_cols(q0, k_cols, k_true):
    return min((q0 // CHUNK + 1) * CHUNK, k_true, k_cols) // LANES * LANES


def _count_ge(ref, cand16, width):
    acc = None
    for t in range(width // LANES):
        one = jnp.where(ref[:, t * LANES:(t + 1) * LANES] >= cand16, jnp.int16(1), jnp.int16(0))
        acc = one if acc is None else acc + one
    return jnp.sum(acc.astype(F32), axis=1, keepdims=True)


def _search16(ref, want, width):
    def body(it, prefix):
        cand = prefix | jnp.left_shift(jnp.int32(1), 15 - it)
        cnt = _count_ge(ref, (cand - 32768).astype(jnp.int16), width)
        return jnp.where(cnt >= want, cand, prefix)
    return lax.fori_loop(0, 16, body, jnp.zeros((ref.shape[0], 1), jnp.int32))


def _select_topk(score, topk, hi_ref, lo_ref):
    q_rows, l = score.shape
    bits = lax.bitcast_convert_type(score, jnp.int32)
    key = bits ^ ((bits >> 31) & jnp.int32(0x7FFFFFFF))
    kf = jnp.float32(topk)
    hi_ref[...] = (key >> 16).astype(jnp.int16)
    hi_thr = _search16(hi_ref, kf, l) - 32768
    hi_thr16 = hi_thr.astype(jnp.int16)
    above = _count_ge(hi_ref, hi_thr16 + jnp.int16(1), l) * jnp.where(hi_thr < 32767, 1.0, 0.0)
    low = ((key & jnp.int32(0xFFFF)) - 32768).astype(jnp.int16)
    lo_ref[...] = jnp.where(hi_ref[...] == hi_thr16, low, jnp.int16(-32768))
    lo_thr = _search16(lo_ref, kf - above, l)
    thr = (hi_thr << 16) | lo_thr
    need = kf - jnp.sum(jnp.where(key > thr, 1.0, 0.0), axis=1, keepdims=True)
    rr = lax.broadcasted_iota(jnp.int32, (LANES, LANES), 0)
    cc = lax.broadcasted_iota(jnp.int32, (LANES, LANES), 1)
    before = jnp.where(rr < cc, 1.0, 0.0).astype(BF16)
    carry = jnp.zeros((q_rows, 1), F32)
    sel = []
    for t in range(l // LANES):
        key_t = key[:, t * LANES:(t + 1) * LANES]
        eq_t = jnp.where(key_t == thr, 1.0, 0.0)
        rank = carry + _dot(eq_t.astype(BF16), before)
        take_tie = jnp.where(rank < need, eq_t, 0.0)
        sel.append(jnp.where(key_t > thr, 1.0, take_tie))
        carry = carry + jnp.sum(eq_t, axis=1, keepdims=True)
    return jnp.concatenate(sel, axis=1)


def _dsa_attend(q0, k_true, topk, iq, iw, bq, ik_pieces, bk_pieces, bv_pieces, o_ref, hi_ref, lo_ref):
    qn = iq.shape[0]
    l = sum(_piece_len(p) for p in ik_pieces)
    score = None
    for h in range(IDX_HEADS):
        hs = slice(h * IDX_DIM, (h + 1) * IDX_DIM)
        dots = _cat([_qk(iq[:, hs], p) for p in ik_pieces])
        term = iw[:, h:h + 1] * jnp.maximum(dots, 0.0)
        score = term if score is None else score + term
    free = _free_cols(q0, l, k_true)
    adm = _admissible(q0, qn, free, l - free, k_true)
    if k_true <= topk:
        tail = jnp.where(adm, 0.0, NEG)
        mask = tail if free == 0 else jnp.concatenate([jnp.zeros((qn, free), F32), tail], axis=1)
    else:
        score = score + 0.0
        tail = jnp.where(adm, score[:, free:], -jnp.inf)
        score = tail if free == 0 else jnp.concatenate([score[:, :free], tail], axis=1)
        sel = _select_topk(score, topk, hi_ref, lo_ref)
        tail = jnp.where(adm, jnp.where(sel[:, free:] > 0.0, 0.0, NEG), NEG)
        mask = tail if free == 0 else jnp.concatenate([jnp.where(sel[:, :free] > 0.0, 0.0, NEG), tail], axis=1)
    for h in range(B_HEADS):
        hs = slice(h * HEAD_DIM, (h + 1) * HEAD_DIM)
        s = _cat([_qk(bq[:, hs], p) for p in bk_pieces]) + mask
        m = s.max(axis=-1, keepdims=True)
        p = jnp.exp2(s - m)
        den = p.sum(axis=-1, keepdims=True)
        o = _pv_pieces(p.astype(BF16), bv_pieces)
        o_ref[0, :, hs] = (o / den).astype(o_ref.dtype)


def _bit_planes(key):
    a = [key[8 * j:8 * (j + 1), :] for j in range(32)]
    j, m = 16, 0x0000FFFF
    while j:
        for k in range(32):
            if k & j:
                continue
            t = (lax.shift_right_logical(a[k], jnp.int32(j)) ^ a[k | j]) & jnp.int32(m - (1 << 32) if m >> 31 else m)
            a[k] = a[k] ^ (t << j)
            a[k | j] = a[k | j] ^ t
        j >>= 1
        m = (m ^ (m << j)) & 0xFFFFFFFF
    return a


BITS_GROUP = 256


def _kth_largest_t(key, topk):
    l, qn = key.shape
    groups = l // BITS_GROUP
    ukey = key ^ jnp.int32(INT_MIN)
    planes = [_bit_planes(ukey[g * BITS_GROUP:(g + 1) * BITS_GROUP, :]) for g in range(groups)]
    alive = [jnp.full((8, qn), -1, jnp.int32) for _ in range(groups)]
    above = jnp.zeros((1, qn), F32)
    thr = jnp.zeros((1, qn), jnp.int32)
    kf = jnp.float32(topk)
    for b in range(31, -1, -1):
        with_bit = [alive[g] & planes[g][b] for g in range(groups)]
        cnt = lax.population_count(with_bit[0])
        for g in range(1, groups):
            cnt = cnt + lax.population_count(with_bit[g])
        c = above + jnp.sum(cnt.astype(F32), axis=0, keepdims=True)
        take = c >= kf
        thr = jnp.where(take, thr | jnp.int32((1 << b) - (1 << 32) if b == 31 else (1 << b)), thr)
        above = jnp.where(take, above, c)
        alive = [jnp.where(take, with_bit[g], alive[g] ^ with_bit[g]) for g in range(groups)]
    return thr ^ jnp.int32(INT_MIN), above


def _select_topk_t(score, topk):
    l, qn = score.shape
    bits = lax.bitcast_convert_type(score, jnp.int32)
    key = bits ^ ((bits >> 31) & jnp.int32(0x7FFFFFFF))
    thr, above = _kth_largest_t(key, topk)
    need = jnp.float32(topk) - above
    rr = lax.broadcasted_iota(jnp.int32, (LANES, LANES), 0)
    cc = lax.broadcasted_iota(jnp.int32, (LANES, LANES), 1)
    earlier = jnp.where(cc < rr, 1.0, 0.0).astype(BF16)
    carry = jnp.zeros((1, qn), F32)
    sel = []
    for t in range(l // LANES):
        key_t = key[t * LANES:(t + 1) * LANES, :]
        eq_t = jnp.where(key_t == thr, 1.0, 0.0)
        rank = carry + _dot(earlier, eq_t.astype(BF16))
        take_tie = jnp.where(rank < need, eq_t, 0.0)
        sel.append(jnp.where(key_t > thr, 1.0, take_tie))
        carry = carry + jnp.sum(eq_t, axis=0, keepdims=True)
    return jnp.concatenate(sel, axis=0)


def _dsa_prompt_kernel(q0, k_len, topk, iqt_ref, iwt_ref, bqt_ref, ik_ref, bk_ref, bvt_ref, o_ref):
    qn = iqt_ref.shape[2]
    ik = ik_ref[0]
    score = None
    for h in range(IDX_HEADS):
        dots = _dot(ik, iqt_ref[0, h * IDX_DIM:(h + 1) * IDX_DIM, :])
        term = iwt_ref[0, h:h + 1, :] * jnp.maximum(dots, 0.0)
        score = term if score is None else score + term
    free = _free_cols(q0, k_len, k_len)
    kpos = lax.broadcasted_iota(jnp.int32, (k_len - free, qn), 0) + free
    qpos = lax.broadcasted_iota(jnp.int32, (k_len - free, qn), 1) + q0
    shift = CHUNK.bit_length() - 1
    adm = (kpos >> shift) <= (qpos >> shift)
    if k_len <= topk:
        tail = jnp.where(adm, 0.0, NEG)
        mask = tail if free == 0 else jnp.concatenate([jnp.zeros((free, qn), F32), tail], axis=0)
    else:
        score = score + 0.0
        tail = jnp.where(adm, score[free:, :], -jnp.inf)
        score = tail if free == 0 else jnp.concatenate([score[:free, :], tail], axis=0)
        sel = _select_topk_t(score, topk)
        tail = jnp.where(adm, jnp.where(sel[free:, :] > 0.0, 0.0, NEG), NEG)
        mask = tail if free == 0 else jnp.concatenate([jnp.where(sel[:free, :] > 0.0, 0.0, NEG), tail], axis=0)
    bk = bk_ref[0]
    outs = []
    for h in range(B_HEADS):
        s = _dot(bk, bqt_ref[0, h * HEAD_DIM:(h + 1) * HEAD_DIM, :]) + mask
        m = s.max(axis=0, keepdims=True)
        p = jnp.exp2(s - m)
        den = p.sum(axis=0, keepdims=True)
        outs.append(_dot(bvt_ref[0], p.astype(BF16)) / den)
    o_ref[0] = jnp.concatenate(outs, axis=0).T.astype(o_ref.dtype)


CAUSAL_QB = 256


def _dsa_prompt(iqt, iwt, bqt, ikb, bkb, bvtb):
    b, _, s = iqt.shape
    qb = CAUSAL_QB
    topk = min(TOPK_MAX, s // 4)
    out = None
    for k in range(s // qb):
        k_len = (k + 1) * qb
        qspec = lambda d, k=k: pl.BlockSpec((1, d, qb), lambda bi: (bi, 0, k))
        rows = pl.BlockSpec((1, k_len, HEAD_DIM), lambda bi: (bi, 0, 0))
        out = _call_into(
            functools.partial(_dsa_prompt_kernel, k * qb, k_len, topk), 6, None if out is None else (out,),
            out_shape=jax.ShapeDtypeStruct((b, s, B_WIDTH), BF16),
            grid=(b,),
            in_specs=[qspec(IDX_HEADS * IDX_DIM), qspec(IDX_HEADS), qspec(B_WIDTH), rows, rows,
                      pl.BlockSpec((1, HEAD_DIM, k_len), lambda bi: (bi, 0, 0))],
            out_specs=pl.BlockSpec((1, qb, B_WIDTH), functools.partial(lambda bi, k: (bi, k, 0), k=k)),
            compiler_params=_params(("parallel",)),
            name="dsa_prompt",
        )(iqt, iwt, bqt, ikb, bkb, bvtb)
    return out


def _dsa_step_kernel(q0, k_true, topk, iq_ref, iw_ref, bq_ref, pik_ref, pbk_ref, pbv_ref,
                     ik_ref, bk_ref, bv_ref, o_ref, hi_ref, lo_ref):
    two = lambda old, new: [("T", old[0].astype(BF16)), ("N", new[0].astype(BF16))]
    _dsa_attend(q0, k_true, topk, iq_ref[0], iw_ref[0], bq_ref[0],
                two(pik_ref, ik_ref), two(pbk_ref, bk_ref), two(pbv_ref, bv_ref), o_ref, hi_ref, lo_ref)


def _dsa_step(iq, iw, bq, pik, pbk, pbv, ik, bk, bv, t_true, first):
    b, t, _ = iq.shape
    p_len = pik.shape[2]
    assert p_len % LANES == 0 and ik.shape[1] % LANES == 0
    k_true = p_len + t_true
    topk = min(TOPK_MAX, k_true // 4)
    qspec = lambda w: pl.BlockSpec((1, t, w), lambda bi: (bi, 0, 0))
    old = pl.BlockSpec((1, HEAD_DIM, p_len), lambda bi: (first + bi, 0, 0))
    new = pl.BlockSpec((1, ik.shape[1], HEAD_DIM), lambda bi: (bi, 0, 0))
    return pl.pallas_call(
        functools.partial(_dsa_step_kernel, p_len, k_true, topk),
        out_shape=jax.ShapeDtypeStruct((b, t, B_WIDTH), BF16),
        grid=(b,),
        in_specs=[qspec(IDX_HEADS * IDX_DIM), qspec(IDX_HEADS), qspec(B_WIDTH), old, old, old, new, new, new],
        out_specs=qspec(B_WIDTH),
        scratch_shapes=[pltpu.VMEM((t, p_len + ik.shape[1]), jnp.int16)] * 2,
        compiler_params=_params(("parallel",)),
        name="dsa_step",
    )(iq, iw, bq, pik, pbk, pbv, ik, bk, bv)


def _diff_attend(q0, k_true, lam_init, q, k_of, v_of, lam_ref, g_ref, o_ref):
    qn = q.shape[0]
    lp = lam_ref[...]
    lam = (jnp.exp(jnp.sum(lp[0:1] * lp[1:2], axis=1, keepdims=True))
           - jnp.exp(jnp.sum(lp[2:3] * lp[3:4], axis=1, keepdims=True)) + lam_init)
    l = sum(_piece_len(p) for p in k_of(0, 0))
    free = _free_cols(q0, l, k_true)
    tail = jnp.where(_admissible(q0, qn, free, l - free, k_true), 0.0, NEG)
    gain = g_ref[...] * (1.0 - lam_init)
    for h in range(C_HEADS):
        es, invs = [], []
        for part in range(2):
            cs = slice((2 * h + part) * HEAD_DIM, (2 * h + part + 1) * HEAD_DIM)
            s = _cat([_qk(q[:, cs], p) for p in k_of(h, part)])
            s = s + tail if free == 0 else jnp.concatenate([s[:, :free], s[:, free:] + tail], axis=1)
            m = s.max(axis=-1, keepdims=True)
            e = jnp.exp2(s - m)
            es.append(e)
            invs.append(1.0 / e.sum(axis=-1, keepdims=True))
        vp = v_of(h)
        o = _pv_pieces(es[0].astype(BF16), vp) * invs[0] - _pv_pieces(es[1].astype(BF16), vp) * (lam * invs[1])
        o = o * lax.rsqrt(jnp.mean(o * o, axis=-1, keepdims=True) + SUBLN_EPS) * gain
        o_ref[0, :, h * 2 * HEAD_DIM:(h + 1) * 2 * HEAD_DIM] = o.astype(o_ref.dtype)


def _diff_prompt_kernel(q0, k_len, lam_init, qt_ref, k_ref, vt_ref, lam_ref, g_ref, o_ref):
    qn = qt_ref.shape[2]
    lp = lam_ref[...]
    lam = (jnp.exp(jnp.sum(lp[0:1] * lp[1:2], axis=1, keepdims=True))
           - jnp.exp(jnp.sum(lp[2:3] * lp[3:4], axis=1, keepdims=True)) + lam_init)
    free = _free_cols(q0, k_len, k_len)
    kpos = lax.broadcasted_iota(jnp.int32, (k_len - free, qn), 0) + free
    qpos = lax.broadcasted_iota(jnp.int32, (k_len - free, qn), 1) + q0
    shift = CHUNK.bit_length() - 1
    tail = jnp.where((kpos >> shift) <= (qpos >> shift), 0.0, NEG)
    gain = g_ref[...] * (1.0 - lam_init)
    outs = []
    for h in range(C_HEADS):
        es, invs = [], []
        for part in range(2):
            cs = slice((2 * h + part) * HEAD_DIM, (2 * h + part + 1) * HEAD_DIM)
            s = _dot(k_ref[0, :, cs], qt_ref[0, cs, :])
            s = s + tail if free == 0 else jnp.concatenate([s[:free, :], s[free:, :] + tail], axis=0)
            m = s.max(axis=0, keepdims=True)
            e = jnp.exp2(s - m)
            es.append(e.astype(BF16))
            invs.append(1.0 / e.sum(axis=0, keepdims=True))
        vt = vt_ref[0, h * 2 * HEAD_DIM:(h + 1) * 2 * HEAD_DIM, :]
        o = _dot(vt, es[0]) * invs[0] - _dot(vt, es[1]) * (lam * invs[1])
        outs.append(o * lax.rsqrt(jnp.mean(o * o, axis=0, keepdims=True) + SUBLN_EPS) * gain)
    o_ref[0] = jnp.concatenate(outs, axis=0).T.astype(o_ref.dtype)


def _lam_init(layer):
    return 0.8 - 0.6 * math.exp(-0.3 * layer)


def _diff_prompt(cqt, ckb, cvtb, lam_p, subln_gc, layer):
    b, _, s = cqt.shape
    qb = CAUSAL_QB
    small = [pl.BlockSpec((4, HEAD_DIM), lambda bi: (0, 0)), pl.BlockSpec((2 * HEAD_DIM, 1), lambda bi: (0, 0))]
    out = None
    for k in range(s // qb):
        k_len = (k + 1) * qb
        out = _call_into(
            functools.partial(_diff_prompt_kernel, k * qb, k_len, _lam_init(layer)), 5, None if out is None else (out,),
            out_shape=jax.ShapeDtypeStruct((b, s, C_WIDTH), BF16),
            grid=(b,),
            in_specs=[pl.BlockSpec((1, C_WIDTH, qb), functools.partial(lambda bi, k: (bi, 0, k), k=k)),
                      pl.BlockSpec((1, k_len, C_WIDTH), lambda bi: (bi, 0, 0)),
                      pl.BlockSpec((1, C_WIDTH, k_len), lambda bi: (bi, 0, 0))] + small,
            out_specs=pl.BlockSpec((1, qb, C_WIDTH), functools.partial(lambda bi, k: (bi, k, 0), k=k)),
            compiler_params=_params(("parallel",)),
            name="diff_prompt",
        )(cqt, ckb, cvtb, lam_p, subln_gc)
    return out


def _diff_step_kernel(q0, k_true, lam_init, q_ref, pk_ref, pv_ref, k_ref, v_ref, lam_ref, g_ref, o_ref):
    p_len = pk_ref.shape[1] // C_HEADS
    kn = k_ref[0].astype(BF16)
    vn = v_ref[0].astype(BF16)

    def k_of(h, part):
        old = pk_ref[0, pl.ds(h, p_len, stride=C_HEADS), :].astype(BF16)
        cs = slice((2 * h + part) * HEAD_DIM, (2 * h + part + 1) * HEAD_DIM)
        return [("N", old[:, part * HEAD_DIM:(part + 1) * HEAD_DIM]), ("N", kn[:, cs])]

    def v_of(h):
        old = pv_ref[0, pl.ds(h, p_len, stride=C_HEADS), :].astype(BF16)
        return [("N", old), ("N", vn[:, h * 2 * HEAD_DIM:(h + 1) * 2 * HEAD_DIM])]

    _diff_attend(q0, k_true, lam_init, q_ref[0], k_of, v_of, lam_ref, g_ref, o_ref)


def _diff_step(cq, pck, pcv, ck, cv, lam_p, subln_g, layer, t_true, first):
    b, t, _ = cq.shape
    p_len = pck.shape[1] // C_HEADS
    assert p_len % LANES == 0 and ck.shape[1] % LANES == 0
    qspec = pl.BlockSpec((1, t, C_WIDTH), lambda bi: (bi, 0, 0))
    old = pl.BlockSpec((1, p_len * C_HEADS, LANES), lambda bi: (first + bi, 0, 0))
    new = pl.BlockSpec((1, ck.shape[1], C_WIDTH), lambda bi: (bi, 0, 0))
    return pl.pallas_call(
        functools.partial(_diff_step_kernel, p_len, p_len + t_true, _lam_init(layer)),
        out_shape=jax.ShapeDtypeStruct((b, t, C_WIDTH), BF16),
        grid=(b,),
        in_specs=[qspec, old, old, new, new,
                  pl.BlockSpec((4, HEAD_DIM), lambda bi: (0, 0)), pl.BlockSpec((1, 2 * HEAD_DIM), lambda bi: (0, 0))],
        out_specs=qspec,
        compiler_params=_params(("parallel",)),
        name="diff_step",
    )(cq, pck, pcv, ck, cv, lam_p, subln_g)


def _merge_kernel(x_ref, ya_ref, yb_ref, yc_ref, wg_ref, wa_ref, wb_ref, wc_ref, wo_ref, g_ref, b_ref, o_ref):
    x = x_ref[...]
    xb = x.astype(BF16)
    merged = None
    for k, (y_ref, w_ref) in enumerate(((ya_ref, wa_ref), (yb_ref, wb_ref), (yc_ref, wc_ref))):
        gate = jax.nn.sigmoid(_dot(xb, wg_ref[:, k * D_MODEL:(k + 1) * D_MODEL]))
        term = gate * _dot(y_ref[...], w_ref[...])
        merged = term if merged is None else merged + term
    out = _dot(merged.astype(BF16), wo_ref[...])
    o_ref[...] = _layer_norm(ALPHA * x + out, g_ref[...], b_ref[...])


def _merge(x, ya, yb, yc, w_gates, wb_a, wb_b, wb_c, w_out, g, b):
    n = x.shape[0]
    tm = _row_tile(n, 512)
    row = lambda w: pl.BlockSpec((tm, w), lambda i: (i, 0))
    full = lambda a: pl.BlockSpec(a.shape, lambda i: (0, 0))
    return pl.pallas_call(
        _merge_kernel,
        out_shape=jax.ShapeDtypeStruct((n, D_MODEL), F32),
        grid=(n // tm,),
        in_specs=[row(D_MODEL), row(A_WIDTH), row(B_WIDTH), row(C_WIDTH),
                  full(w_gates), full(wb_a), full(wb_b), full(wb_c), full(w_out), full(g), full(b)],
        out_specs=row(D_MODEL),
        compiler_params=_params(("parallel",)),
        name="merge",
    )(x, ya, yb, yc, w_gates, wb_a, wb_b, wb_c, w_out, g, b)


MEM_TILES = MEM_HEAD_DIM // LANES
MEM_ROWS = MEM_HEADS * MEM_TILES


def _mem_kernel(x_ref, mk_ref, mv_ref, wq_ref, wo_ref, g_ref, b_ref, o_ref, mkb_ref, mvb_ref):
    i = pl.program_id(1)

    @pl.when(i == 0)
    def _():
        for h in range(MEM_HEADS):
            for c in range(MEM_TILES):
                rows = pl.ds(c * MEM_HEADS + h, N_MEM, stride=MEM_ROWS)
                mkb_ref[h, :, c * LANES:(c + 1) * LANES] = mk_ref[0, rows, :].astype(BF16)
                mvb_ref[h, :, c * LANES:(c + 1) * LANES] = mv_ref[0, rows, :].astype(BF16)

    x = x_ref[0]
    q = _dot(x.astype(BF16), wq_ref[...]).astype(BF16)
    scale = MEM_HEAD_DIM ** -0.5
    heads = []
    for h in range(MEM_HEADS):
        hs = slice(h * MEM_HEAD_DIM, (h + 1) * MEM_HEAD_DIM)
        s = _dot_nt(q[:, hs], mkb_ref[h]) * scale
        m = s.max(axis=-1, keepdims=True)
        p = jnp.exp(s - m)
        den = p.sum(axis=-1, keepdims=True)
        heads.append((_dot(p.astype(BF16), mvb_ref[h]) / den).astype(BF16))
    o = _dot(jnp.concatenate(heads, axis=1), wo_ref[...])
    o_ref[0] = _layer_norm(ALPHA * x + o, g_ref[...], b_ref[...])


def _mem_attn(x, mem_k, mem_v, first, w_q, w_o, g, b):
    bsz, t, _ = x.shape
    tq = _row_tile(t, 512)
    xspec = pl.BlockSpec((1, tq, D_MODEL), lambda bi, i: (bi, i, 0))
    mspec = pl.BlockSpec((1, N_MEM * MEM_ROWS, LANES), lambda bi, i: (first + bi, 0, 0))
    full = lambda a: pl.BlockSpec(a.shape, lambda bi, i: (0, 0))
    return pl.pallas_call(
        _mem_kernel,
        out_shape=jax.ShapeDtypeStruct((bsz, t, D_MODEL), F32),
        grid=(bsz, t // tq),
        in_specs=[xspec, mspec, mspec, full(w_q), full(w_o), full(g), full(b)],
        out_specs=xspec,
        scratch_shapes=[pltpu.VMEM((MEM_HEADS, N_MEM, MEM_HEAD_DIM), BF16)] * 2,
        compiler_params=_params(("parallel", "arbitrary")),
        name="mem_attn",
    )(x, mem_k, mem_v, w_q, w_o, g, b)


def _store_mem_rows(o_ref, v):
    rows = v.shape[0]
    for h in range(MEM_HEADS):
        for c in range(MEM_TILES):
            col = h * MEM_HEAD_DIM + c * LANES
            o_ref[pl.ds(c * MEM_HEADS + h, rows, stride=MEM_ROWS), :] = v[:, col:col + LANES]


def _memproj_kernel(x_ref, wk_ref, wv_ref, k_ref, v_ref):
    xb = x_ref[...].astype(BF16)
    _store_mem_rows(k_ref, _dot(xb, wk_ref[...]))
    _store_mem_rows(v_ref, _dot(xb, wv_ref[...]))


def _memproj(mem, w_k, w_v, layer, depth, prev):
    n = mem.shape[0]
    tm = _row_tile(n, 512)
    nblk = n // tm
    row = pl.BlockSpec((tm, D_MODEL), lambda i: (i, 0))
    full = pl.BlockSpec((D_MODEL, D_MODEL), lambda i: (0, 0))
    out = pl.BlockSpec((tm * MEM_ROWS, LANES), lambda i: (layer * nblk + i, 0))
    return _call_into(
        _memproj_kernel, 3, prev,
        out_shape=(jax.ShapeDtypeStruct((depth * n * MEM_ROWS, LANES), F32),) * 2,
        grid=(n // tm,),
        in_specs=[row, full, full],
        out_specs=(out, out),
        compiler_params=_params(("parallel",)),
        name="memproj",
    )(mem, w_k, w_v)


def _pad_rows(a, rows):
    pad = rows - a.shape[1]
    return a if pad == 0 else jnp.pad(a, ((0, 0), (0, pad), (0, 0)))


def _rest_of_layer(x2, ya, yb, yc, mem_k, mem_v, mem_first, b, t, w):
    n = b * t
    x3 = _merge(x2, ya.reshape(n, -1), yb.reshape(n, -1), yc.reshape(n, -1),
                w["gates"], w["wb_a"], w["wb_b"], w["wb_c"], w["w_out"], w["ln_g"][1:2], w["ln_b"][1:2])
    x4 = _mem_attn(x3.reshape(b, t, D_MODEL), mem_k, mem_v, mem_first, w["wm_q"], w["wm_o"],
                   w["ln_g"][2:3], w["ln_b"][2:3])
    x5 = _ffn(x4.reshape(n, D_MODEL), w["f2_gu"], w["f2_d"], w["ln_g"][3:4], w["ln_b"][3:4])
    return x5.reshape(b, t, D_MODEL)


def _prompt_layer(x, mem_k, mem_v, layer, depth, prev_c, w):
    b, s, _ = x.shape
    n = b * s
    x2 = _ffn(x.reshape(n, D_MODEL), w["f1_gu"], w["f1_d"], w["ln_g"][0:1], w["ln_b"][0:1])
    tab = _rope_tables(jnp.arange(s, dtype=jnp.int32))
    (ck4, cv4, aq, akt, avt, aktb, avtb, bqt, bkt, bvt, ikt, bkb, bvtb, ikb,
     iqt, iwt, cqt, ckb, cvtb) = _inproj_prompt(x2, w["in_pack"], tab, b, s, layer, depth, prev_c)
    seq = lambda a: a.reshape(b, s, a.shape[-1])
    ya = _band_prompt(seq(aq), aktb, avtb, w["rel_bias"])
    yb = _dsa_prompt(iqt, iwt, bqt, seq(ikb), seq(bkb), bvtb)
    yc = _diff_prompt(cqt, seq(ckb), cvtb, w["lam"], w["subln_g"].reshape(-1, 1), layer)
    y = _rest_of_layer(x2, ya, yb, yc, mem_k, mem_v, layer * b, b, s, w)
    keep = min(A_WINDOW, s)
    heads_last = lambda a: jnp.transpose(a[:, :, s - keep:].reshape(b, A_HEADS, HEAD_DIM, keep), (0, 3, 1, 2))
    rows_last = lambda a: jnp.swapaxes(a, 1, 2)
    state = (heads_last(akt), heads_last(avt), rows_last(bkt), rows_last(bvt), rows_last(ikt))
    return y, state, (ck4, cv4)


def _step_layer(x, p_len, past, mem_k, mem_v, layer, w):
    b, t, _ = x.shape
    first = layer * b
    n = b * t
    x2 = _ffn(x.reshape(n, D_MODEL), w["f1_gu"], w["f1_d"], w["ln_g"][0:1], w["ln_b"][0:1])
    tm = _row_tile(n, INPROJ_TM)
    assert tm % t == 0
    tab = jnp.tile(_rope_tables(p_len + jnp.arange(t, dtype=jnp.int32)), (1, 1, tm // t, 1))
    outs = _inproj_step(x2, w["in_pack"], tab)
    aq, ak, av, bq, bk, bv, iq, ik, iw, cq, ck, cv = [o.reshape(b, t, o.shape[-1]) for o in outs]
    pa_k, pa_v, pb_k, pb_v, pb_i, pc_k, pc_v = past
    ya = _band_step(aq, ak, av, pa_k, pa_v, w["rel_bias"], first)
    t_pad = -(-t // LANES) * LANES
    pad = lambda a: _pad_rows(a, t_pad)
    yb = _dsa_step(iq, iw, bq, pb_i, pb_k, pb_v, pad(ik), pad(bk), pad(bv), t, first)
    yc = _diff_step(cq, pc_k, pc_v, pad(ck), pad(cv), w["lam"], w["subln_g"], layer, t, first)
    y = _rest_of_layer(x2, ya, yb, yc, mem_k, mem_v, first, b, t, w)
    state = (ak.reshape(b, t, A_HEADS, HEAD_DIM), av.reshape(b, t, A_HEADS, HEAD_DIM), bk, bv, ik,
             ck.reshape(b, t, C_HEADS, 2 * HEAD_DIM), cv.reshape(b, t, C_HEADS, 2 * HEAD_DIM))
    return y, state


def _layer_weights(l, ln_g, ln_b, ffn1_w_gu, ffn1_w_d, ffn2_w_gu, ffn2_w_d, w_in, a_rel_bias, c_lambda,
                   c_subln_g, w_branch_a, w_branch_b, w_branch_c, w_out, w_mem_q, w_mem_k, w_mem_v, w_mem_o):
    wi = w_in[l]
    iw_end = IN_OFFS[9]
    gates_at = IN_OFFS[12]
    in_pack = jnp.concatenate(
        [wi[:, :iw_end], jnp.zeros((D_MODEL, IW_PAD), wi.dtype), wi[:, iw_end:gates_at]], axis=1)
    bf = lambda a: a.astype(BF16)
    return dict(
        ln_g=ln_g[l].astype(F32), ln_b=ln_b[l].astype(F32),
        f1_gu=bf(ffn1_w_gu[l]), f1_d=bf(ffn1_w_d[l]), f2_gu=bf(ffn2_w_gu[l]), f2_d=bf(ffn2_w_d[l]),
        in_pack=bf(in_pack), gates=bf(wi[:, gates_at:]),
        rel_bias=a_rel_bias[l], lam=c_lambda[l].astype(F32), subln_g=c_subln_g[l].astype(F32)[None, :],
        wb_a=bf(w_branch_a[l]), wb_b=bf(w_branch_b[l]), wb_c=bf(w_branch_c[l]), w_out=bf(w_out[l]),
        wm_q=bf(w_mem_q[l]), wm_k=bf(w_mem_k[l]), wm_v=bf(w_mem_v[l]), wm_o=bf(w_mem_o[l]),
    )


def kernel(x_prompt, x_sample, cache_a_k, cache_a_v, cache_b_k, cache_b_v, cache_b_idx, cache_c_k, cache_c_v, cache_mem_k, cache_mem_v, mem_prompt, ln_g, ln_b, ffn1_w_gu, ffn1_w_d, ffn2_w_gu, ffn2_w_d, w_in, a_rel_bias, c_lambda, c_subln_g, w_branch_a, w_branch_b, w_branch_c, w_out, w_mem_q, w_mem_k, w_mem_v, w_mem_o):
    bp, s, _ = x_prompt.shape
    bs, t, _ = x_sample.shape
    p_len = cache_b_k.shape[2]
    depth = ln_g.shape[0]
    yp, ys = x_prompt, x_sample
    st_p, st_s = [], []
    rows = depth * bs
    keys_last = lambda a: jnp.transpose(a, (0, 1, 3, 2)).reshape(rows, a.shape[3], a.shape[2])
    pairs = lambda a: a.reshape(rows, a.shape[2] * C_HEADS, 2 * HEAD_DIM)
    past = (cache_a_k.reshape(rows, -1, A_WIDTH), cache_a_v.reshape(rows, -1, A_WIDTH),
            keys_last(cache_b_k), keys_last(cache_b_v), keys_last(cache_b_idx), pairs(cache_c_k), pairs(cache_c_v))
    mem_rows = lambda a: a.reshape(rows, N_MEM, MEM_HEADS, MEM_TILES, LANES).transpose(0, 1, 3, 2, 4).reshape(
        rows, N_MEM * MEM_ROWS, LANES)
    mem_k_s, mem_v_s = mem_rows(cache_mem_k), mem_rows(cache_mem_v)
    mem_p = c_p = None
    for l in range(depth):
        w = _layer_weights(l, ln_g, ln_b, ffn1_w_gu, ffn1_w_d, ffn2_w_gu, ffn2_w_d, w_in, a_rel_bias,
                           c_lambda, c_subln_g, w_branch_a, w_branch_b, w_branch_c, w_out,
                           w_mem_q, w_mem_k, w_mem_v, w_mem_o)
        mem_p = _memproj(mem_prompt.reshape(bp * N_MEM, D_MODEL), w["wm_k"], w["wm_v"], l, depth, mem_p)
        mk, mv = (a.reshape(depth * bp, N_MEM * MEM_ROWS, LANES) for a in mem_p)
        yp, sp, c_p = _prompt_layer(yp, mk, mv, l, depth, c_p, w)
        ys, ss = _step_layer(ys, p_len, past, mem_k_s, mem_v_s, l, w)
        st_p.append(sp)
        st_s.append(ss)
    stk = lambda sts: [jnp.stack([st[i] for st in sts]) for i in range(len(sts[0]))]
    (a_k_p, a_v_p, b_k_p, b_v_p, b_i_p) = stk(st_p)
    (a_k_s, a_v_s, b_k_s, b_v_s, b_i_s, c_k_s, c_v_s) = stk(st_s)
    c_k_p, c_v_p = (a.reshape(depth, bp, s, C_HEADS, 2 * HEAD_DIM) for a in c_p)
    mem_out = lambda a: a.reshape(depth, bp, N_MEM, MEM_TILES, MEM_HEADS, LANES).transpose(0, 1, 2, 4, 3, 5).reshape(
        depth, bp, N_MEM, MEM_HEADS, MEM_HEAD_DIM)
    return (yp, ys, a_k_p, a_v_p, b_k_p, b_v_p, b_i_p, c_k_p, c_v_p,
            mem_out(mem_p[0]), mem_out(mem_p[1]), a_k_s, a_v_s, b_k_s, b_v_s, b_i_s, c_k_s, c_v_s)
```

```python
import functools
import math

import jax
import jax.numpy as jnp
import numpy as np
from jax import lax
from jax.experimental import pallas as pl
from jax.experimental.pallas import tpu as pltpu

F32 = jnp.float32
BF16 = jnp.bfloat16

D_MODEL = 1024
CHUNK = 64
HEAD_DIM = 64
ROT_DIM = HEAD_DIM // 4
ROT_HALF = ROT_DIM // 2
ROPE_THETA = 500000.0
A_HEADS = 4
A_LEFT_CHUNKS = 8
A_WINDOW = A_LEFT_CHUNKS * CHUNK
REL_CLIP = 128
B_HEADS = 4
IDX_HEADS = 8
IDX_DIM = 64
TOPK_MAX = 256
C_HEADS = 4
N_MEM = 256
MEM_HEADS = 4
MEM_HEAD_DIM = D_MODEL // MEM_HEADS
D_FF = ((8 * D_MODEL // 3 + 127) // 128) * 128
N_BRANCH = 3
DEPTH = 2
ALPHA = (2.0 * DEPTH) ** 0.25
LN_EPS = 1e-5
SUBLN_EPS = 1e-5
A_WIDTH = A_HEADS * HEAD_DIM
B_WIDTH = B_HEADS * HEAD_DIM
C_WIDTH = C_HEADS * 2 * HEAD_DIM
IN_SIZES = (A_WIDTH, A_WIDTH, A_WIDTH,
            B_WIDTH, HEAD_DIM, HEAD_DIM, IDX_HEADS * IDX_DIM, IDX_DIM, IDX_HEADS,
            C_WIDTH, C_WIDTH, C_WIDTH,
            N_BRANCH * D_MODEL)
IN_OFFS = tuple(int(v) for v in np.cumsum((0,) + IN_SIZES))

LANES = 128
V7X_VMEM_LIMIT = 56 * 1024 * 1024

NEG = -1e30
LOG2E = math.log2(math.e)
INT_MIN = -2 ** 31

IW_PAD = LANES - HEAD_DIM - IDX_HEADS
P_AQ, P_AK, P_AV = 0, 256, 512
P_B = 768
P_I = 1152
P_C = 1792
P_END = 3328


def _dot(a, b):
    return jnp.dot(a, b, preferred_element_type=F32)


def _dot_nt(a, b):
    return lax.dot_general(a, b, (((1,), (1,)), ((), ())), preferred_element_type=F32)


def _layer_norm(v, g, b):
    mu = jnp.mean(v, axis=-1, keepdims=True)
    d = v - mu
    var = jnp.mean(d * d, axis=-1, keepdims=True)
    return d * lax.rsqrt(var + LN_EPS) * g + b


def _params(sem):
    return pltpu.CompilerParams(dimension_semantics=sem, vmem_limit_bytes=V7X_VMEM_LIMIT)


def _row_tile(n, want):
    t = min(n, want)
    assert n % t == 0
    return t


def _call_into(body, n_in, prev, **kw):
    if prev is None:
        return pl.pallas_call(body, **kw)
    n_prev = len(prev)
    kernel = lambda *refs: body(*refs[:n_in], *refs[n_in + n_prev:])
    kw["in_specs"] = list(kw["in_specs"]) + [pl.BlockSpec(memory_space=pl.ANY)] * n_prev
    call = pl.pallas_call(kernel, input_output_aliases={n_in + i: i for i in range(n_prev)}, **kw)
    return lambda *args: call(*args, *prev)


def _ffn_kernel(x_ref, wg_ref, wu_ref, wd_ref, g_ref, b_ref, o_ref, xb_ref, acc_ref):
    j = pl.program_id(1)

    @pl.when(j == 0)
    def _():
        xb_ref[...] = x_ref[...].astype(BF16)
        acc_ref[...] = jnp.zeros_like(acc_ref)

    xb = xb_ref[...]
    g = _dot(xb, wg_ref[...])
    u = _dot(xb, wu_ref[...])
    h = g * jax.nn.sigmoid(g) * u
    acc_ref[...] += _dot(h.astype(BF16), wd_ref[...])

    @pl.when(j == pl.num_programs(1) - 1)
    def _():
        o_ref[...] = _layer_norm(ALPHA * x_ref[...] + 0.5 * acc_ref[...], g_ref[...], b_ref[...])


def _ffn(x, w_gu, w_d, g, b):
    n = x.shape[0]
    tm = _row_tile(n, 1024)
    tf = 256
    nf = D_FF // tf
    return pl.pallas_call(
        _ffn_kernel,
        out_shape=jax.ShapeDtypeStruct((n, D_MODEL), F32),
        grid=(n // tm, nf),
        in_specs=[
            pl.BlockSpec((tm, D_MODEL), lambda i, j: (i, 0)),
            pl.BlockSpec((D_MODEL, tf), lambda i, j: (0, j)),
            pl.BlockSpec((D_MODEL, tf), lambda i, j: (0, j + nf)),
            pl.BlockSpec((tf, D_MODEL), lambda i, j: (j, 0)),
            pl.BlockSpec((1, D_MODEL), lambda i, j: (0, 0)),
            pl.BlockSpec((1, D_MODEL), lambda i, j: (0, 0)),
        ],
        out_specs=pl.BlockSpec((tm, D_MODEL), lambda i, j: (i, 0)),
        scratch_shapes=[pltpu.VMEM((tm, D_MODEL), BF16), pltpu.VMEM((tm, D_MODEL), F32)],
        compiler_params=_params(("parallel", "arbitrary")),
        name="ffn",
    )(x, w_gu, w_gu, w_d, g, b)


def _rope_tables(pos):
    t = pos.shape[0]
    inv_freq = ROPE_THETA ** (-jnp.arange(ROT_HALF, dtype=F32) / ROT_HALF)
    ang = pos.astype(F32)[:, None] * inv_freq
    cos, sin = jnp.cos(ang), jnp.sin(ang)
    rest = HEAD_DIM - ROT_DIM
    ones = lambda w: jnp.ones((t, w), F32)
    zeros = lambda w: jnp.zeros((t, w), F32)
    c64 = jnp.concatenate([cos, cos, ones(rest)], axis=1)
    lo64 = jnp.concatenate([-sin, zeros(ROT_HALF), zeros(rest)], axis=1)
    hi64 = jnp.concatenate([zeros(ROT_HALF), sin, zeros(rest)], axis=1)
    iw64 = jnp.concatenate([jnp.full((t, IDX_HEADS), IDX_HEADS ** -0.5, F32), ones(HEAD_DIM - IDX_HEADS)], axis=1)
    cat = lambda a, b_: jnp.concatenate([a, b_], axis=1)
    both = jnp.stack([cat(c64, c64), cat(lo64, lo64), cat(hi64, hi64)])
    upper_id = jnp.stack([cat(c64, ones(HEAD_DIM)), cat(lo64, zeros(HEAD_DIM)), cat(hi64, zeros(HEAD_DIM))])
    upper_iw = jnp.stack([cat(c64, iw64), cat(lo64, zeros(HEAD_DIM)), cat(hi64, zeros(HEAD_DIM))])
    return jnp.stack([both, upper_id, upper_iw])


def _rope_apply(z, tab_ref, variant):
    cos = tab_ref[variant, 0]
    lo = tab_ref[variant, 1]
    hi = tab_ref[variant, 2]
    outs = []
    for k in range(z.shape[1] // LANES):
        zc = z[:, k * LANES:(k + 1) * LANES]
        outs.append(zc * cos + pltpu.roll(zc, LANES - ROT_HALF, 1) * lo + pltpu.roll(zc, ROT_HALF, 1) * hi)
    return outs[0] if len(outs) == 1 else jnp.concatenate(outs, axis=1)


def _inproj_sections(x_ref, w_ref, tab_ref):
    xb = x_ref[...].astype(BF16)
    za = _dot(xb, w_ref[:, P_AQ:P_B])
    zb = _dot(xb, w_ref[:, P_B:P_I])
    zi = _dot(xb, w_ref[:, P_I:P_C])
    zc = _dot(xb, w_ref[:, P_C:P_END])
    qs = HEAD_DIM ** -0.5 * LOG2E
    return dict(
        aq=za[:, 0:256] * qs, ak=za[:, 256:512], av=za[:, 512:768],
        bq=_rope_apply(zb[:, 0:256], tab_ref, 0) * qs,
        bkv=_rope_apply(zb[:, 256:384], tab_ref, 1),
        iq=_rope_apply(zi[:, 0:512], tab_ref, 0),
        ikw=_rope_apply(zi[:, 512:640], tab_ref, 2),
        cq=_rope_apply(zc[:, 0:512], tab_ref, 0) * qs,
        ck=_rope_apply(zc[:, 512:1024], tab_ref, 0),
        cv=zc[:, 1024:1536])


def _store_heads_interleaved(o_ref, v, heads):
    rows = v.shape[0]
    for h in range(heads):
        o_ref[pl.ds(h, rows, stride=heads), :] = v[:, h * LANES:(h + 1) * LANES]


def _inproj_prompt_kernel(x_ref, w_ref, tab_ref, ck4_ref, cv4_ref, akt_ref, avt_ref, bkt_ref, bvt_ref, ikt_ref,
                          aq_ref, aktb_ref, avtb_ref,
                          bqt_ref, bkb_ref, bvtb_ref, ikb_ref,
                          iqt_ref, iwt_ref, cq_ref, cktb_ref, cvb_ref):
    s = _inproj_sections(x_ref, w_ref, tab_ref)
    aq_ref[...] = s["aq"].astype(BF16)
    akt = s["ak"].T
    avt = s["av"].T
    akt_ref[0] = akt
    avt_ref[0] = avt
    aktb_ref[0] = akt.astype(BF16)
    avtb_ref[0] = avt.astype(BF16)
    bqt_ref[0] = s["bq"].T.astype(BF16)
    kvt = s["bkv"].T
    bkt_ref[0] = kvt[0:HEAD_DIM]
    bvt_ref[0] = kvt[HEAD_DIM:2 * HEAD_DIM]
    bkb_ref[...] = s["bkv"][:, 0:HEAD_DIM].astype(BF16)
    bvtb_ref[0] = kvt[HEAD_DIM:2 * HEAD_DIM].astype(BF16)
    kwt = s["ikw"].T
    ikt_ref[0] = kwt[0:IDX_DIM]
    ikb_ref[...] = s["ikw"][:, 0:IDX_DIM].astype(BF16)
    iqt_ref[0] = s["iq"].T.astype(BF16)
    iwt_ref[0] = kwt[IDX_DIM:IDX_DIM + IDX_HEADS]
    cq_ref[...] = s["cq"].astype(BF16)
    _store_heads_interleaved(ck4_ref, s["ck"], C_HEADS)
    _store_heads_interleaved(cv4_ref, s["cv"], C_HEADS)
    cktb_ref[0] = s["ck"].T.astype(BF16)
    cvb_ref[...] = s["cv"].astype(BF16)


INPROJ_TM = 256


def _inproj_prompt(x, w_pack, tab, bsz, seq, layer, depth, prev):
    n = x.shape[0]
    tm = INPROJ_TM
    spb = seq // tm
    nblk = n // tm
    tok = lambda w: pl.BlockSpec((tm, w), lambda i: (i, 0))
    tr = lambda d: pl.BlockSpec((1, d, tm), lambda i: (i // spb, 0, i % spb))
    sds = jax.ShapeDtypeStruct
    tok_o = lambda w, dt: (sds((n, w), dt), tok(w))
    tr_o = lambda d, dt: (sds((bsz, d, seq), dt), tr(d))
    il_o = (sds((depth * n * C_HEADS, LANES), F32),
            pl.BlockSpec((tm * C_HEADS, LANES), lambda i: (layer * nblk + i, 0)))
    tr_all = lambda d: (sds((depth * bsz, d, seq), F32),
                        pl.BlockSpec((1, d, tm), lambda i: (layer * bsz + i // spb, 0, i % spb)))
    outs = [il_o, il_o, tr_all(A_WIDTH), tr_all(A_WIDTH), tr_all(HEAD_DIM), tr_all(HEAD_DIM), tr_all(IDX_DIM),
            tok_o(A_WIDTH, BF16), tr_o(A_WIDTH, BF16), tr_o(A_WIDTH, BF16),
            tr_o(B_WIDTH, BF16),
            tok_o(HEAD_DIM, BF16), tr_o(HEAD_DIM, BF16), tok_o(IDX_DIM, BF16),
            tr_o(IDX_HEADS * IDX_DIM, BF16), tr_o(IDX_HEADS, F32), tok_o(C_WIDTH, BF16),
            tr_o(C_WIDTH, BF16), tok_o(C_WIDTH, BF16)]
    return _call_into(
        _inproj_prompt_kernel, 3, prev,
        out_shape=tuple(o[0] for o in outs),
        grid=(n // tm,),
        in_specs=[tok(D_MODEL),
                  pl.BlockSpec((D_MODEL, P_END), lambda i: (0, 0)),
                  pl.BlockSpec((3, 3, tm, LANES), lambda i: (0, 0, i % spb, 0))],
        out_specs=tuple(o[1] for o in outs),
        compiler_params=_params(("parallel",)),
        name="inproj_prompt",
    )(x, w_pack, tab)


def _inproj_step_kernel(x_ref, w_ref, tab_ref,
                        aq_ref, ak_ref, av_ref, bq_ref, bk_ref, bv_ref,
                        iq_ref, ik_ref, iw_ref, cq_ref, ck_ref, cv_ref):
    s = _inproj_sections(x_ref, w_ref, tab_ref)
    aq_ref[...] = s["aq"].astype(BF16)
    ak_ref[...] = s["ak"]
    av_ref[...] = s["av"]
    bq_ref[...] = s["bq"].astype(BF16)
    bk_ref[...] = s["bkv"][:, 0:HEAD_DIM]
    bv_ref[...] = s["bkv"][:, HEAD_DIM:2 * HEAD_DIM]
    iq_ref[...] = s["iq"].astype(BF16)
    ik_ref[...] = s["ikw"][:, 0:IDX_DIM]
    iw_ref[...] = s["ikw"][:, IDX_DIM:IDX_DIM + IDX_HEADS]
    cq_ref[...] = s["cq"].astype(BF16)
    ck_ref[...] = s["ck"]
    cv_ref[...] = s["cv"]


def _inproj_step(x, w_pack, tab):
    n = x.shape[0]
    tm = _row_tile(n, INPROJ_TM)
    widths = ((A_WIDTH, BF16), (A_WIDTH, F32), (A_WIDTH, F32), (B_WIDTH, BF16), (HEAD_DIM, F32), (HEAD_DIM, F32),
              (IDX_HEADS * IDX_DIM, BF16), (IDX_DIM, F32), (IDX_HEADS, F32),
              (C_WIDTH, BF16), (C_WIDTH, F32), (C_WIDTH, F32))
    return pl.pallas_call(
        _inproj_step_kernel,
        out_shape=tuple(jax.ShapeDtypeStruct((n, w), dt) for w, dt in widths),
        grid=(n // tm,),
        in_specs=[
            pl.BlockSpec((tm, D_MODEL), lambda i: (i, 0)),
            pl.BlockSpec((D_MODEL, P_END), lambda i: (0, 0)),
            pl.BlockSpec((3, 3, tm, LANES), lambda i: (0, 0, 0, 0)),
        ],
        out_specs=tuple(pl.BlockSpec((tm, w), lambda i: (i, 0)) for w, _ in widths),
        compiler_params=_params(("parallel",)),
        name="inproj_step",
    )(x, w_pack, tab)


def _qk(q, piece):
    kind, k = piece
    return _dot(q, k) if kind == "T" else _dot_nt(q, k)


def _pv(p, piece):
    kind, v = piece
    return _dot_nt(p, v) if kind == "T" else _dot(p, v)


def _piece_len(piece):
    kind, a = piece
    return a.shape[1] if kind == "T" else a.shape[0]


def _cat(parts):
    return parts[0] if len(parts) == 1 else jnp.concatenate(parts, axis=1)


def _pv_pieces(p, pieces):
    off = 0
    o = None
    for piece in pieces:
        n = _piece_len(piece)
        t = _pv(p[:, off:off + n], piece)
        o = t if o is None else o + t
        off += n
    return o


def _toeplitz_bias(rel_bias, q_rows, k_cols, q_off):
    period = pl.next_power_of_2(q_rows + k_cols)
    e = np.arange(period)
    e = np.where(e >= k_cols, e - period, e)
    idx = np.clip(q_off - e, -REL_CLIP, REL_CLIP) + REL_CLIP
    table = rel_bias[:, idx].astype(F32) * LOG2E
    flat = jnp.tile(table, (1, q_rows + 1))[:, :q_rows * (period - 1)]
    return flat.reshape(rel_bias.shape[0], q_rows, period - 1)[:, :, :k_cols]


def _band_attend(q, pieces, bias_ref, o_ref):
    for h in range(A_HEADS):
        hs = slice(h * HEAD_DIM, (h + 1) * HEAD_DIM)
        qh = q[:, hs]
        ss = []
        off = 0
        for k_of, _, extra in pieces:
            kp = k_of(h)
            n = _piece_len(kp)
            s = _qk(qh, kp) + bias_ref[h, :, off:off + n]
            if extra is not None:
                s = s + extra
            ss.append(s)
            off += n
        m = ss[0].max(axis=-1, keepdims=True)
        for s in ss[1:]:
            m = jnp.maximum(m, s.max(axis=-1, keepdims=True))
        l = jnp.zeros_like(m)
        o = jnp.zeros((q.shape[0], HEAD_DIM), F32)
        for s, (_, v_of, _) in zip(ss, pieces):
            p = jnp.exp2(s - m)
            l = l + p.sum(axis=-1, keepdims=True)
            o = o + _pv(p.astype(BF16), v_of(h))
        o_ref[0, :, hs] = (o / l).astype(o_ref.dtype)


BAND_QB = 256
BAND_KB = 3


def _band_prompt_kernel(q_ref, k0_ref, k1_ref, k2_ref, v0_ref, v1_ref, v2_ref, bias_ref, o_ref):
    j = pl.program_id(1)
    pieces = []
    for kb, (k_ref, v_ref) in enumerate(((k0_ref, v0_ref), (k1_ref, v1_ref), (k2_ref, v2_ref))):
        back = BAND_KB - 1 - kb
        extra = None if back == 0 else jnp.where(j >= back, 0.0, NEG)
        head_rows = lambda ref: (lambda h: ("T", ref[0, h * HEAD_DIM:(h + 1) * HEAD_DIM, :]))
        pieces.append((head_rows(k_ref), head_rows(v_ref), extra))
    _band_attend(q_ref[0], pieces, bias_ref, o_ref)


def _band_prompt(aq, akt, avt, rel_bias):
    b, s, _ = aq.shape
    assert A_WINDOW == (BAND_KB - 1) * BAND_QB and s % BAND_QB == 0
    kw = BAND_KB * BAND_QB
    r = np.arange(BAND_QB)[:, None] + A_WINDOW
    c = np.arange(kw)[None, :]
    in_band = (c // CHUNK <= r // CHUNK) & (c // CHUNK >= r // CHUNK - A_LEFT_CHUNKS)
    bias = jnp.where(in_band[None], _toeplitz_bias(rel_bias, BAND_QB, kw, A_WINDOW), NEG)
    qspec = pl.BlockSpec((1, BAND_QB, A_WIDTH), lambda bi, j: (bi, j, 0))
    kspecs = [pl.BlockSpec((1, A_WIDTH, BAND_QB),
                           functools.partial(lambda bi, j, back: (bi, 0, jnp.maximum(j - back, 0)),
                                             back=BAND_KB - 1 - kb))
              for kb in range(BAND_KB)]
    return pl.pallas_call(
        _band_prompt_kernel,
        out_shape=jax.ShapeDtypeStruct((b, s, A_WIDTH), BF16),
        grid=(b, s // BAND_QB),
        in_specs=[qspec] + kspecs + kspecs + [pl.BlockSpec((A_HEADS, BAND_QB, kw), lambda bi, j: (0, 0, 0))],
        out_specs=qspec,
        compiler_params=_params(("parallel", "parallel")),
        name="band_prompt",
    )(aq, akt, akt, akt, avt, avt, avt, bias)


def _band_step_kernel(q_ref, pk_ref, k_ref, pv_ref, v_ref, bias_ref, o_ref):
    head_cols = lambda a: (lambda h: ("N", a[:, h * HEAD_DIM:(h + 1) * HEAD_DIM]))
    pieces = [(head_cols(pk_ref[0].astype(BF16)), head_cols(pv_ref[0].astype(BF16)), None),
              (head_cols(k_ref[0].astype(BF16)), head_cols(v_ref[0].astype(BF16)), None)]
    _band_attend(q_ref[0], pieces, bias_ref, o_ref)


def _band_step(aq, ak, av, pk, pv, rel_bias, first):
    b, t, _ = aq.shape
    w = pk.shape[1]
    bias = _toeplitz_bias(rel_bias, t, w + t, w)
    new = pl.BlockSpec((1, t, A_WIDTH), lambda bi: (bi, 0, 0))
    old = pl.BlockSpec((1, w, A_WIDTH), lambda bi: (first + bi, 0, 0))
    return pl.pallas_call(
        _band_step_kernel,
        out_shape=jax.ShapeDtypeStruct((b, t, A_WIDTH), BF16),
        grid=(b,),
        in_specs=[new, old, new, old, new, pl.BlockSpec((A_HEADS, t, w + t), lambda bi: (0, 0, 0))],
        out_specs=new,
        compiler_params=_params(("parallel",)),
        name="band_step",
    )(aq, pk, ak, pv, av, bias)


def _admissible(q0, q_rows, k0, k_cols, k_true):
    kpos = lax.broadcasted_iota(jnp.int32, (q_rows, k_cols), 1) + k0
    qpos = lax.broadcasted_iota(jnp.int32, (q_rows, k_cols), 0) + q0
    shift = CHUNK.bit_length() - 1
    return ((kpos >> shift) <= (qpos >> shift)) & (kpos < k_true)


def _free_cols(q0, k_cols, k_true):
    return min((q0 // CHUNK + 1) * CHUNK, k_true, k_cols) // LANES * LANES


def _count_ge(ref, cand16, width):
    acc = None
    for t in range(width // LANES):
        one = jnp.where(ref[:, t * LANES:(t + 1) * LANES] >= cand16, jnp.int16(1), jnp.int16(0))
        acc = one if acc is None else acc + one
    return jnp.sum(acc.astype(F32), axis=1, keepdims=True)


def _search16(ref, want, width):
    def body(it, prefix):
        cand = prefix | jnp.left_shift(jnp.int32(1), 15 - it)
        cnt = _count_ge(ref, (cand - 32768).astype(jnp.int16), width)
        return jnp.where(cnt >= want, cand, prefix)
    return lax.fori_loop(0, 16, body, jnp.zeros((ref.shape[0], 1), jnp.int32))


def _select_topk(score, topk, hi_ref, lo_ref):
    q_rows, l = score.shape
    bits = lax.bitcast_convert_type(score, jnp.int32)
    key = bits ^ ((bits >> 31) & jnp.int32(0x7FFFFFFF))
    kf = jnp.float32(topk)
    hi_ref[...] = (key >> 16).astype(jnp.int16)
    hi_thr = _search16(hi_ref, kf, l) - 32768
    hi_thr16 = hi_thr.astype(jnp.int16)
    above = _count_ge(hi_ref, hi_thr16 + jnp.int16(1), l) * jnp.where(hi_thr < 32767, 1.0, 0.0)
    low = ((key & jnp.int32(0xFFFF)) - 32768).astype(jnp.int16)
    lo_ref[...] = jnp.where(hi_ref[...] == hi_thr16, low, jnp.int16(-32768))
    lo_thr = _search16(lo_ref, kf - above, l)
    thr = (hi_thr << 16) | lo_thr
    need = kf - jnp.sum(jnp.where(key > thr, 1.0, 0.0), axis=1, keepdims=True)
    rr = lax.broadcasted_iota(jnp.int32, (LANES, LANES), 0)
    cc = lax.broadcasted_iota(jnp.int32, (LANES, LANES), 1)
    before = jnp.where(rr < cc, 1.0, 0.0).astype(BF16)
    carry = jnp.zeros((q_rows, 1), F32)
    sel = []
    for t in range(l // LANES):
        key_t = key[:, t * LANES:(t + 1) * LANES]
        eq_t = jnp.where(key_t == thr, 1.0, 0.0)
        rank = carry + _dot(eq_t.astype(BF16), before)
        take_tie = jnp.where(rank < need, eq_t, 0.0)
        sel.append(jnp.where(key_t > thr, 1.0, take_tie))
        carry = carry + jnp.sum(eq_t, axis=1, keepdims=True)
    return jnp.concatenate(sel, axis=1)


def _dsa_attend(q0, k_true, topk, iq, iw, bq, ik_pieces, bk_pieces, bv_pieces, o_ref, hi_ref, lo_ref):
    qn = iq.shape[0]
    l = sum(_piece_len(p) for p in ik_pieces)
    score = None
    for h in range(IDX_HEADS):
        hs = slice(h * IDX_DIM, (h + 1) * IDX_DIM)
        dots = _cat([_qk(iq[:, hs], p) for p in ik_pieces])
        term = iw[:, h:h + 1] * jnp.maximum(dots, 0.0)
        score = term if score is None else score + term
    free = _free_cols(q0, l, k_true)
    adm = _admissible(q0, qn, free, l - free, k_true)
    if k_true <= topk:
        tail = jnp.where(adm, 0.0, NEG)
        mask = tail if free == 0 else jnp.concatenate([jnp.zeros((qn, free), F32), tail], axis=1)
    else:
        score = score + 0.0
        tail = jnp.where(adm, score[:, free:], -jnp.inf)
        score = tail if free == 0 else jnp.concatenate([score[:, :free], tail], axis=1)
        sel = _select_topk(score, topk, hi_ref, lo_ref)
        tail = jnp.where(adm, jnp.where(sel[:, free:] > 0.0, 0.0, NEG), NEG)
        mask = tail if free == 0 else jnp.concatenate([jnp.where(sel[:, :free] > 0.0, 0.0, NEG), tail], axis=1)
    for h in range(B_HEADS):
        hs = slice(h * HEAD_DIM, (h + 1) * HEAD_DIM)
        s = _cat([_qk(bq[:, hs], p) for p in bk_pieces]) + mask
        m = s.max(axis=-1, keepdims=True)
        p = jnp.exp2(s - m)
        den = p.sum(axis=-1, keepdims=True)
        o = _pv_pieces(p.astype(BF16), bv_pieces)
        o_ref[0, :, hs] = (o / den).astype(o_ref.dtype)


def _bit_planes(key):
    a = [key[8 * j:8 * (j + 1), :] for j in range(32)]
    j, m = 16, 0x0000FFFF
    while j:
        for k in range(32):
            if k & j:
                continue
            t = (lax.shift_right_logical(a[k], jnp.int32(j)) ^ a[k | j]) & jnp.int32(m - (1 << 32) if m >> 31 else m)
            a[k] = a[k] ^ (t << j)
            a[k | j] = a[k | j] ^ t
        j >>= 1
        m = (m ^ (m << j)) & 0xFFFFFFFF
    return a


BITS_GROUP = 256


def _kth_largest_t(key, topk):
    l, qn = key.shape
    groups = l // BITS_GROUP
    ukey = key ^ jnp.int32(INT_MIN)
    planes = [_bit_planes(ukey[g * BITS_GROUP:(g + 1) * BITS_GROUP, :]) for g in range(groups)]
    alive = [jnp.full((8, qn), -1, jnp.int32) for _ in range(groups)]
    above = jnp.zeros((1, qn), F32)
    thr = jnp.zeros((1, qn), jnp.int32)
    kf = jnp.float32(topk)
    for b in range(31, -1, -1):
        with_bit = [alive[g] & planes[g][b] for g in range(groups)]
        cnt = lax.population_count(with_bit[0])
        for g in range(1, groups):
            cnt = cnt + lax.population_count(with_bit[g])
        c = above + jnp.sum(cnt.astype(F32), axis=0, keepdims=True)
        take = c >= kf
        thr = jnp.where(take, thr | jnp.int32((1 << b) - (1 << 32) if b == 31 else (1 << b)), thr)
        above = jnp.where(take, above, c)
        alive = [jnp.where(take, with_bit[g], alive[g] ^ with_bit[g]) for g in range(groups)]
    return thr ^ jnp.int32(INT_MIN), above


def _select_topk_t(score, topk):
    l, qn = score.shape
    bits = lax.bitcast_convert_type(score, jnp.int32)
    key = bits ^ ((bits >> 31) & jnp.int32(0x7FFFFFFF))
    thr, above = _kth_largest_t(key, topk)
    need = jnp.float32(topk) - above
    rr = lax.broadcasted_iota(jnp.int32, (LANES, LANES), 0)
    cc = lax.broadcasted_iota(jnp.int32, (LANES, LANES), 1)
    earlier = jnp.where(cc < rr, 1.0, 0.0).astype(BF16)
    carry = jnp.zeros((1, qn), F32)
    sel = []
    for t in range(l // LANES):
        key_t = key[t * LANES:(t + 1) * LANES, :]
        eq_t = jnp.where(key_t == thr, 1.0, 0.0)
        rank = carry + _dot(earlier, eq_t.astype(BF16))
        take_tie = jnp.where(rank < need, eq_t, 0.0)
        sel.append(jnp.where(key_t > thr, 1.0, take_tie))
        carry = carry + jnp.sum(eq_t, axis=0, keepdims=True)
    return jnp.concatenate(sel, axis=0)


def _dsa_prompt_kernel(q0, k_len, topk, iqt_ref, iwt_ref, bqt_ref, ik_ref, bk_ref, bvt_ref, o_ref):
    qn = iqt_ref.shape[2]
    ik = ik_ref[0]
    score = None
    for h in range(IDX_HEADS):
        dots = _dot(ik, iqt_ref[0, h * IDX_DIM:(h + 1) * IDX_DIM, :])
        term = iwt_ref[0, h:h + 1, :] * jnp.maximum(dots, 0.0)
        score = term if score is None else score + term
    free = _free_cols(q0, k_len, k_len)
    kpos = lax.broadcasted_iota(jnp.int32, (k_len - free, qn), 0) + free
    qpos = lax.broadcasted_iota(jnp.int32, (k_len - free, qn), 1) + q0
    shift = CHUNK.bit_length() - 1
    adm = (kpos >> shift) <= (qpos >> shift)
    if k_len <= topk:
        tail = jnp.where(adm, 0.0, NEG)
        mask = tail if free == 0 else jnp.concatenate([jnp.zeros((free, qn), F32), tail], axis=0)
    else:
        score = score + 0.0
        tail = jnp.where(adm, score[free:, :], -jnp.inf)
        score = tail if free == 0 else jnp.concatenate([score[:free, :], tail], axis=0)
        sel = _select_topk_t(score, topk)
        tail = jnp.where(adm, jnp.where(sel[free:, :] > 0.0, 0.0, NEG), NEG)
        mask = tail if free == 0 else jnp.concatenate([jnp.where(sel[:free, :] > 0.0, 0.0, NEG), tail], axis=0)
    bk = bk_ref[0]
    outs = []
    for h in range(B_HEADS):
        s = _dot(bk, bqt_ref[0, h * HEAD_DIM:(h + 1) * HEAD_DIM, :]) + mask
        m = s.max(axis=0, keepdims=True)
        p = jnp.exp2(s - m)
        den = p.sum(axis=0, keepdims=True)
        outs.append(_dot(bvt_ref[0], p.astype(BF16)) / den)
    o_ref[0] = jnp.concatenate(outs, axis=0).T.astype(o_ref.dtype)


CAUSAL_QB = 256


def _dsa_prompt(iqt, iwt, bqt, ikb, bkb, bvtb):
    b, _, s = iqt.shape
    qb = CAUSAL_QB
    topk = min(TOPK_MAX, s // 4)
    out = None
    for k in range(s // qb):
        k_len = (k + 1) * qb
        qspec = lambda d, k=k: pl.BlockSpec((1, d, qb), lambda bi: (bi, 0, k))
        rows = pl.BlockSpec((1, k_len, HEAD_DIM), lambda bi: (bi, 0, 0))
        out = _call_into(
            functools.partial(_dsa_prompt_kernel, k * qb, k_len, topk), 6, None if out is None else (out,),
            out_shape=jax.ShapeDtypeStruct((b, s, B_WIDTH), BF16),
            grid=(b,),
            in_specs=[qspec(IDX_HEADS * IDX_DIM), qspec(IDX_HEADS), qspec(B_WIDTH), rows, rows,
                      pl.BlockSpec((1, HEAD_DIM, k_len), lambda bi: (bi, 0, 0))],
            out_specs=pl.BlockSpec((1, qb, B_WIDTH), functools.partial(lambda bi, k: (bi, k, 0), k=k)),
            compiler_params=_params(("parallel",)),
            name="dsa_prompt",
        )(iqt, iwt, bqt, ikb, bkb, bvtb)
    return out


def _dsa_step_kernel(q0, k_true, topk, iq_ref, iw_ref, bq_ref, pik_ref, pbk_ref, pbv_ref,
                     ik_ref, bk_ref, bv_ref, o_ref, hi_ref, lo_ref):
    two = lambda old, new: [("T", old[0].astype(BF16)), ("N", new[0].astype(BF16))]
    _dsa_attend(q0, k_true, topk, iq_ref[0], iw_ref[0], bq_ref[0],
                two(pik_ref, ik_ref), two(pbk_ref, bk_ref), two(pbv_ref, bv_ref), o_ref, hi_ref, lo_ref)


def _dsa_step(iq, iw, bq, pik, pbk, pbv, ik, bk, bv, t_true, first):
    b, t, _ = iq.shape
    p_len = pik.shape[2]
    assert p_len % LANES == 0 and ik.shape[1] % LANES == 0
    k_true = p_len + t_true
    topk = min(TOPK_MAX, k_true // 4)
    qspec = lambda w: pl.BlockSpec((1, t, w), lambda bi: (bi, 0, 0))
    old = pl.BlockSpec((1, HEAD_DIM, p_len), lambda bi: (first + bi, 0, 0))
    new = pl.BlockSpec((1, ik.shape[1], HEAD_DIM), lambda bi: (bi, 0, 0))
    return pl.pallas_call(
        functools.partial(_dsa_step_kernel, p_len, k_true, topk),
        out_shape=jax.ShapeDtypeStruct((b, t, B_WIDTH), BF16),
        grid=(b,),
        in_specs=[qspec(IDX_HEADS * IDX_DIM), qspec(IDX_HEADS), qspec(B_WIDTH), old, old, old, new, new, new],
        out_specs=qspec(B_WIDTH),
        scratch_shapes=[pltpu.VMEM((t, p_len + ik.shape[1]), jnp.int16)] * 2,
        compiler_params=_params(("parallel",)),
        name="dsa_step",
    )(iq, iw, bq, pik, pbk, pbv, ik, bk, bv)


def _diff_attend(q0, k_true, lam_init, q, k_of, v_of, lam_ref, g_ref, o_ref):
    qn = q.shape[0]
    lp = lam_ref[...]
    lam = (jnp.exp(jnp.sum(lp[0:1] * lp[1:2], axis=1, keepdims=True))
           - jnp.exp(jnp.sum(lp[2:3] * lp[3:4], axis=1, keepdims=True)) + lam_init)
    l = sum(_piece_len(p) for p in k_of(0, 0))
    free = _free_cols(q0, l, k_true)
    tail = jnp.where(_admissible(q0, qn, free, l - free, k_true), 0.0, NEG)
    gain = g_ref[...] * (1.0 - lam_init)
    for h in range(C_HEADS):
        es, invs = [], []
        for part in range(2):
            cs = slice((2 * h + part) * HEAD_DIM, (2 * h + part + 1) * HEAD_DIM)
            s = _cat([_qk(q[:, cs], p) for p in k_of(h, part)])
            s = s + tail if free == 0 else jnp.concatenate([s[:, :free], s[:, free:] + tail], axis=1)
            m = s.max(axis=-1, keepdims=True)
            e = jnp.exp2(s - m)
            es.append(e)
            invs.append(1.0 / e.sum(axis=-1, keepdims=True))
        vp = v_of(h)
        o = _pv_pieces(es[0].astype(BF16), vp) * invs[0] - _pv_pieces(es[1].astype(BF16), vp) * (lam * invs[1])
        o = o * lax.rsqrt(jnp.mean(o * o, axis=-1, keepdims=True) + SUBLN_EPS) * gain
        o_ref[0, :, h * 2 * HEAD_DIM:(h + 1) * 2 * HEAD_DIM] = o.astype(o_ref.dtype)


def _diff_prompt_kernel(q0, k_len, lam_init, q_ref, kt_ref, v_ref, lam_ref, g_ref, o_ref):
    k_of = lambda h, part: [("T", kt_ref[0, (2 * h + part) * HEAD_DIM:(2 * h + part + 1) * HEAD_DIM, :])]
    v_of = lambda h: [("N", v_ref[0, :, h * 2 * HEAD_DIM:(h + 1) * 2 * HEAD_DIM])]
    _diff_attend(q0, k_len, lam_init, q_ref[0], k_of, v_of, lam_ref, g_ref, o_ref)


def _lam_init(layer):
    return 0.8 - 0.6 * math.exp(-0.3 * layer)


def _diff_prompt(cq, cktb, cvb, lam_p, subln_g, layer):
    b, s, _ = cq.shape
    qb = CAUSAL_QB
    small = [pl.BlockSpec((4, HEAD_DIM), lambda bi: (0, 0)), pl.BlockSpec((1, 2 * HEAD_DIM), lambda bi: (0, 0))]
    out = None
    for k in range(s // qb):
        k_len = (k + 1) * qb
        out = _call_into(
            functools.partial(_diff_prompt_kernel, k * qb, k_len, _lam_init(layer)), 5, None if out is None else (out,),
            out_shape=jax.ShapeDtypeStruct((b, s, C_WIDTH), BF16),
            grid=(b,),
            in_specs=[pl.BlockSpec((1, qb, C_WIDTH), functools.partial(lambda bi, k: (bi, k, 0), k=k)),
                      pl.BlockSpec((1, C_WIDTH, k_len), lambda bi: (bi, 0, 0)),
                      pl.BlockSpec((1, k_len, C_WIDTH), lambda bi: (bi, 0, 0))] + small,
            out_specs=pl.BlockSpec((1, qb, C_WIDTH), functools.partial(lambda bi, k: (bi, k, 0), k=k)),
            compiler_params=_params(("parallel",)),
            name="diff_prompt",
        )(cq, cktb, cvb, lam_p, subln_g)
    return out


def _diff_step_kernel(q0, k_true, lam_init, q_ref, pk_ref, pv_ref, k_ref, v_ref, lam_ref, g_ref, o_ref):
    p_len = pk_ref.shape[1] // C_HEADS
    kn = k_ref[0].astype(BF16)
    vn = v_ref[0].astype(BF16)

    def k_of(h, part):
        old = pk_ref[0, pl.ds(h, p_len, stride=C_HEADS), :].astype(BF16)
        cs = slice((2 * h + part) * HEAD_DIM, (2 * h + part + 1) * HEAD_DIM)
        return [("N", old[:, part * HEAD_DIM:(part + 1) * HEAD_DIM]), ("N", kn[:, cs])]

    def v_of(h):
        old = pv_ref[0, pl.ds(h, p_len, stride=C_HEADS), :].astype(BF16)
        return [("N", old), ("N", vn[:, h * 2 * HEAD_DIM:(h + 1) * 2 * HEAD_DIM])]

    _diff_attend(q0, k_true, lam_init, q_ref[0], k_of, v_of, lam_ref, g_ref, o_ref)


def _diff_step(cq, pck, pcv, ck, cv, lam_p, subln_g, layer, t_true, first):
    b, t, _ = cq.shape
    p_len = pck.shape[1] // C_HEADS
    assert p_len % LANES == 0 and ck.shape[1] % LANES == 0
    qspec = pl.BlockSpec((1, t, C_WIDTH), lambda bi: (bi, 0, 0))
    old = pl.BlockSpec((1, p_len * C_HEADS, LANES), lambda bi: (first + bi, 0, 0))
    new = pl.BlockSpec((1, ck.shape[1], C_WIDTH), lambda bi: (bi, 0, 0))
    return pl.pallas_call(
        functools.partial(_diff_step_kernel, p_len, p_len + t_true, _lam_init(layer)),
        out_shape=jax.ShapeDtypeStruct((b, t, C_WIDTH), BF16),
        grid=(b,),
        in_specs=[qspec, old, old, new, new,
                  pl.BlockSpec((4, HEAD_DIM), lambda bi: (0, 0)), pl.BlockSpec((1, 2 * HEAD_DIM), lambda bi: (0, 0))],
        out_specs=qspec,
        compiler_params=_params(("parallel",)),
        name="diff_step",
    )(cq, pck, pcv, ck, cv, lam_p, subln_g)


def _merge_kernel(x_ref, ya_ref, yb_ref, yc_ref, wg_ref, wa_ref, wb_ref, wc_ref, wo_ref, g_ref, b_ref, o_ref):
    x = x_ref[...]
    xb = x.astype(BF16)
    merged = None
    for k, (y_ref, w_ref) in enumerate(((ya_ref, wa_ref), (yb_ref, wb_ref), (yc_ref, wc_ref))):
        gate = jax.nn.sigmoid(_dot(xb, wg_ref[:, k * D_MODEL:(k + 1) * D_MODEL]))
        term = gate * _dot(y_ref[...], w_ref[...])
        merged = term if merged is None else merged + term
    out = _dot(merged.astype(BF16), wo_ref[...])
    o_ref[...] = _layer_norm(ALPHA * x + out, g_ref[...], b_ref[...])


def _merge(x, ya, yb, yc, w_gates, wb_a, wb_b, wb_c, w_out, g, b):
    n = x.shape[0]
    tm = _row_tile(n, 512)
    row = lambda w: pl.BlockSpec((tm, w), lambda i: (i, 0))
    full = lambda a: pl.BlockSpec(a.shape, lambda i: (0, 0))
    return pl.pallas_call(
        _merge_kernel,
        out_shape=jax.ShapeDtypeStruct((n, D_MODEL), F32),
        grid=(n // tm,),
        in_specs=[row(D_MODEL), row(A_WIDTH), row(B_WIDTH), row(C_WIDTH),
                  full(w_gates), full(wb_a), full(wb_b), full(wb_c), full(w_out), full(g), full(b)],
        out_specs=row(D_MODEL),
        compiler_params=_params(("parallel",)),
        name="merge",
    )(x, ya, yb, yc, w_gates, wb_a, wb_b, wb_c, w_out, g, b)


MEM_TILES = MEM_HEAD_DIM // LANES
MEM_ROWS = MEM_HEADS * MEM_TILES


def _mem_kernel(x_ref, mk_ref, mv_ref, wq_ref, wo_ref, g_ref, b_ref, o_ref, mkb_ref, mvb_ref):
    i = pl.program_id(1)

    @pl.when(i == 0)
    def _():
        for h in range(MEM_HEADS):
            for c in range(MEM_TILES):
                rows = pl.ds(c * MEM_HEADS + h, N_MEM, stride=MEM_ROWS)
                mkb_ref[h, :, c * LANES:(c + 1) * LANES] = mk_ref[0, rows, :].astype(BF16)
                mvb_ref[h, :, c * LANES:(c + 1) * LANES] = mv_ref[0, rows, :].astype(BF16)

    x = x_ref[0]
    q = _dot(x.astype(BF16), wq_ref[...]).astype(BF16)
    scale = MEM_HEAD_DIM ** -0.5
    heads = []
    for h in range(MEM_HEADS):
        hs = slice(h * MEM_HEAD_DIM, (h + 1) * MEM_HEAD_DIM)
        s = _dot_nt(q[:, hs], mkb_ref[h]) * scale
        m = s.max(axis=-1, keepdims=True)
        p = jnp.exp(s - m)
        den = p.sum(axis=-1, keepdims=True)
        heads.append((_dot(p.astype(BF16), mvb_ref[h]) / den).astype(BF16))
    o = _dot(jnp.concatenate(heads, axis=1), wo_ref[...])
    o_ref[0] = _layer_norm(ALPHA * x + o, g_ref[...], b_ref[...])


def _mem_attn(x, mem_k, mem_v, first, w_q, w_o, g, b):
    bsz, t, _ = x.shape
    tq = _row_tile(t, 512)
    xspec = pl.BlockSpec((1, tq, D_MODEL), lambda bi, i: (bi, i, 0))
    mspec = pl.BlockSpec((1, N_MEM * MEM_ROWS, LANES), lambda bi, i: (first + bi, 0, 0))
    full = lambda a: pl.BlockSpec(a.shape, lambda bi, i: (0, 0))
    return pl.pallas_call(
        _mem_kernel,
        out_shape=jax.ShapeDtypeStruct((bsz, t, D_MODEL), F32),
        grid=(bsz, t // tq),
        in_specs=[xspec, mspec, mspec, full(w_q), full(w_o), full(g), full(b)],
        out_specs=xspec,
        scratch_shapes=[pltpu.VMEM((MEM_HEADS, N_MEM, MEM_HEAD_DIM), BF16)] * 2,
        compiler_params=_params(("parallel", "arbitrary")),
        name="mem_attn",
    )(x, mem_k, mem_v, w_q, w_o, g, b)


def _store_mem_rows(o_ref, v):
    rows = v.shape[0]
    for h in range(MEM_HEADS):
        for c in range(MEM_TILES):
            col = h * MEM_HEAD_DIM + c * LANES
            o_ref[pl.ds(c * MEM_HEADS + h, rows, stride=MEM_ROWS), :] = v[:, col:col + LANES]


def _memproj_kernel(x_ref, wk_ref, wv_ref, k_ref, v_ref):
    xb = x_ref[...].astype(BF16)
    _store_mem_rows(k_ref, _dot(xb, wk_ref[...]))
    _store_mem_rows(v_ref, _dot(xb, wv_ref[...]))


def _memproj(mem, w_k, w_v, layer, depth, prev):
    n = mem.shape[0]
    tm = _row_tile(n, 512)
    nblk = n // tm
    row = pl.BlockSpec((tm, D_MODEL), lambda i: (i, 0))
    full = pl.BlockSpec((D_MODEL, D_MODEL), lambda i: (0, 0))
    out = pl.BlockSpec((tm * MEM_ROWS, LANES), lambda i: (layer * nblk + i, 0))
    return _call_into(
        _memproj_kernel, 3, prev,
        out_shape=(jax.ShapeDtypeStruct((depth * n * MEM_ROWS, LANES), F32),) * 2,
        grid=(n // tm,),
        in_specs=[row, full, full],
        out_specs=(out, out),
        compiler_params=_params(("parallel",)),
        name="memproj",
    )(mem, w_k, w_v)


def _pad_rows(a, rows):
    pad = rows - a.shape[1]
    return a if pad == 0 else jnp.pad(a, ((0, 0), (0, pad), (0, 0)))


def _rest_of_layer(x2, ya, yb, yc, mem_k, mem_v, mem_first, b, t, w):
    n = b * t
    x3 = _merge(x2, ya.reshape(n, -1), yb.reshape(n, -1), yc.reshape(n, -1),
                w["gates"], w["wb_a"], w["wb_b"], w["wb_c"], w["w_out"], w["ln_g"][1:2], w["ln_b"][1:2])
    x4 = _mem_attn(x3.reshape(b, t, D_MODEL), mem_k, mem_v, mem_first, w["wm_q"], w["wm_o"],
                   w["ln_g"][2:3], w["ln_b"][2:3])
    x5 = _ffn(x4.reshape(n, D_MODEL), w["f2_gu"], w["f2_d"], w["ln_g"][3:4], w["ln_b"][3:4])
    return x5.reshape(b, t, D_MODEL)


def _prompt_layer(x, mem_k, mem_v, layer, depth, prev_state, w):
    b, s, _ = x.shape
    n = b * s
    x2 = _ffn(x.reshape(n, D_MODEL), w["f1_gu"], w["f1_d"], w["ln_g"][0:1], w["ln_b"][0:1])
    tab = _rope_tables(jnp.arange(s, dtype=jnp.int32))
    (ck4, cv4, akt, avt, bkt, bvt, ikt, aq, aktb, avtb, bqt, bkb, bvtb, ikb,
     iqt, iwt, cq, cktb, cvb) = _inproj_prompt(x2, w["in_pack"], tab, b, s, layer, depth, prev_state)
    seq = lambda a: a.reshape(b, s, a.shape[-1])
    ya = _band_prompt(seq(aq), aktb, avtb, w["rel_bias"])
    yb = _dsa_prompt(iqt, iwt, bqt, seq(ikb), seq(bkb), bvtb)
    yc = _diff_prompt(seq(cq), cktb, seq(cvb), w["lam"], w["subln_g"], layer)
    y = _rest_of_layer(x2, ya, yb, yc, mem_k, mem_v, layer * b, b, s, w)
    return y, (ck4, cv4, akt, avt, bkt, bvt, ikt)


def _step_layer(x, p_len, past, mem_k, mem_v, layer, w):
    b, t, _ = x.shape
    first = layer * b
    n = b * t
    x2 = _ffn(x.reshape(n, D_MODEL), w["f1_gu"], w["f1_d"], w["ln_g"][0:1], w["ln_b"][0:1])
    tm = _row_tile(n, INPROJ_TM)
    assert tm % t == 0
    tab = jnp.tile(_rope_tables(p_len + jnp.arange(t, dtype=jnp.int32)), (1, 1, tm // t, 1))
    outs = _inproj_step(x2, w["in_pack"], tab)
    aq, ak, av, bq, bk, bv, iq, ik, iw, cq, ck, cv = [o.reshape(b, t, o.shape[-1]) for o in outs]
    pa_k, pa_v, pb_k, pb_v, pb_i, pc_k, pc_v = past
    ya = _band_step(aq, ak, av, pa_k, pa_v, w["rel_bias"], first)
    t_pad = -(-t // LANES) * LANES
    pad = lambda a: _pad_rows(a, t_pad)
    yb = _dsa_step(iq, iw, bq, pb_i, pb_k, pb_v, pad(ik), pad(bk), pad(bv), t, first)
    yc = _diff_step(cq, pc_k, pc_v, pad(ck), pad(cv), w["lam"], w["subln_g"], layer, t, first)
    y = _rest_of_layer(x2, ya, yb, yc, mem_k, mem_v, first, b, t, w)
    state = (ak.reshape(b, t, A_HEADS, HEAD_DIM), av.reshape(b, t, A_HEADS, HEAD_DIM), bk, bv, ik,
             ck.reshape(b, t, C_HEADS, 2 * HEAD_DIM), cv.reshape(b, t, C_HEADS, 2 * HEAD_DIM))
    return y, state


def _layer_weights(l, ln_g, ln_b, ffn1_w_gu, ffn1_w_d, ffn2_w_gu, ffn2_w_d, w_in, a_rel_bias, c_lambda,
                   c_subln_g, w_branch_a, w_branch_b, w_branch_c, w_out, w_mem_q, w_mem_k, w_mem_v, w_mem_o):
    wi = w_in[l]
    iw_end = IN_OFFS[9]
    gates_at = IN_OFFS[12]
    in_pack = jnp.concatenate(
        [wi[:, :iw_end], jnp.zeros((D_MODEL, IW_PAD), wi.dtype), wi[:, iw_end:gates_at]], axis=1)
    bf = lambda a: a.astype(BF16)
    return dict(
        ln_g=ln_g[l].astype(F32), ln_b=ln_b[l].astype(F32),
        f1_gu=bf(ffn1_w_gu[l]), f1_d=bf(ffn1_w_d[l]), f2_gu=bf(ffn2_w_gu[l]), f2_d=bf(ffn2_w_d[l]),
        in_pack=bf(in_pack), gates=bf(wi[:, gates_at:]),
        rel_bias=a_rel_bias[l], lam=c_lambda[l].astype(F32), subln_g=c_subln_g[l].astype(F32)[None, :],
        wb_a=bf(w_branch_a[l]), wb_b=bf(w_branch_b[l]), wb_c=bf(w_branch_c[l]), w_out=bf(w_out[l]),
        wm_q=bf(w_mem_q[l]), wm_k=bf(w_mem_k[l]), wm_v=bf(w_mem_v[l]), wm_o=bf(w_mem_o[l]),
    )


def kernel(x_prompt, x_sample, cache_a_k, cache_a_v, cache_b_k, cache_b_v, cache_b_idx, cache_c_k, cache_c_v, cache_mem_k, cache_mem_v, mem_prompt, ln_g, ln_b, ffn1_w_gu, ffn1_w_d, ffn2_w_gu, ffn2_w_d, w_in, a_rel_bias, c_lambda, c_subln_g, w_branch_a, w_branch_b, w_branch_c, w_out, w_mem_q, w_mem_k, w_mem_v, w_mem_o):
    bp, s, _ = x_prompt.shape
    bs, t, _ = x_sample.shape
    p_len = cache_b_k.shape[2]
    depth = ln_g.shape[0]
    yp, ys = x_prompt, x_sample
    st_s = []
    rows = depth * bs
    keys_last = lambda a: jnp.transpose(a, (0, 1, 3, 2)).reshape(rows, a.shape[3], a.shape[2])
    pairs = lambda a: a.reshape(rows, a.shape[2] * C_HEADS, 2 * HEAD_DIM)
    past = (cache_a_k.reshape(rows, -1, A_WIDTH), cache_a_v.reshape(rows, -1, A_WIDTH),
            keys_last(cache_b_k), keys_last(cache_b_v), keys_last(cache_b_idx), pairs(cache_c_k), pairs(cache_c_v))
    mem_rows = lambda a: a.reshape(rows, N_MEM, MEM_HEADS, MEM_TILES, LANES).transpose(0, 1, 3, 2, 4).reshape(
        rows, N_MEM * MEM_ROWS, LANES)
    mem_k_s, mem_v_s = mem_rows(cache_mem_k), mem_rows(cache_mem_v)
    mem_p = state_p = None
    for l in range(depth):
        w = _layer_weights(l, ln_g, ln_b, ffn1_w_gu, ffn1_w_d, ffn2_w_gu, ffn2_w_d, w_in, a_rel_bias,
                           c_lambda, c_subln_g, w_branch_a, w_branch_b, w_branch_c, w_out,
                           w_mem_q, w_mem_k, w_mem_v, w_mem_o)
        mem_p = _memproj(mem_prompt.reshape(bp * N_MEM, D_MODEL), w["wm_k"], w["wm_v"], l, depth, mem_p)
        mk, mv = (a.reshape(depth * bp, N_MEM * MEM_ROWS, LANES) for a in mem_p)
        yp, state_p = _prompt_layer(yp, mk, mv, l, depth, state_p, w)
        ys, ss = _step_layer(ys, p_len, past, mem_k_s, mem_v_s, l, w)
        st_s.append(ss)
    stk = lambda sts: [jnp.stack([st[i] for st in sts]) for i in range(len(sts[0]))]
    (a_k_s, a_v_s, b_k_s, b_v_s, b_i_s, c_k_s, c_v_s) = stk(st_s)
    ck4, cv4, akt, avt, bkt, bvt, ikt = state_p
    c_k_p, c_v_p = (a.reshape(depth, bp, s, C_HEADS, 2 * HEAD_DIM) for a in (ck4, cv4))
    keep = min(A_WINDOW, s)
    heads_last = lambda a: jnp.transpose(
        a[:, :, s - keep:].reshape(depth, bp, A_HEADS, HEAD_DIM, keep), (0, 1, 4, 2, 3))
    rows_last = lambda a: jnp.swapaxes(a.reshape(depth, bp, a.shape[1], s), 2, 3)
    a_k_p, a_v_p = heads_last(akt), heads_last(avt)
    b_k_p, b_v_p, b_i_p = rows_last(bkt), rows_last(bvt), rows_last(ikt)
    mem_out = lambda a: a.reshape(depth, bp, N_MEM, MEM_TILES, MEM_HEADS, LANES).transpose(0, 1, 2, 4, 3, 5).reshape(
        depth, bp, N_MEM, MEM_HEADS, MEM_HEAD_DIM)
    return (yp, ys, a_k_p, a_v_p, b_k_p, b_v_p, b_i_p, c_k_p, c_v_p,
            mem_out(mem_p[0]), mem_out(mem_p[1]), a_k_s, a_v_s, b_k_s, b_v_s, b_i_s, c_k_s, c_v_s)
```
